```python
import jax, jax.numpy as jnp
from jax import lax
import numpy as np

D_MODEL = 1024
BATCH = 4
SEQ = 4096
DEPTH = 2

N_MIXERS = 2
EPS = 1e-6
ROPE_THETA = 500000.0
GM_CHUNK = 128
GM_WIDTH = 2 * D_MODEL
GM_GROUPS = 8
GM_GROUP_DIM = GM_WIDTH // GM_GROUPS
N_HEADS = 8
HEAD_DIM = D_MODEL // N_HEADS
N_KV_HEADS = 2
ROT_DIM = HEAD_DIM // 4
IDX_HEADS = 8
IDX_DIM = 64
IDX_ROT_DIM = IDX_DIM // 4
TOPK_MAX = 256
Q_BLOCK = 128
B_IN_SIZES = (N_HEADS * HEAD_DIM, N_KV_HEADS * HEAD_DIM, N_KV_HEADS * HEAD_DIM,
              IDX_HEADS * IDX_DIM, IDX_DIM, IDX_HEADS)
B_IN_DIM = sum(B_IN_SIZES)
N_EXPERT_GROUPS = 4
EXPERTS_PER_GROUP = 8
N_EXPERTS = N_EXPERT_GROUPS * EXPERTS_PER_GROUP
EXPERT_TOP_K = 2
D_EXPERT = D_MODEL // 2
MOE_BLOCK = 128
PLE_DIM = 256
N_A_LAYERS = (DEPTH + 1) // 2
N_B_LAYERS = DEPTH // 2

kernel_name = 'hybrid_gmlp_dsa_hmoe'


def rmsnorm(x, g):
    xf = x.astype(jnp.float32)
    y = xf * lax.rsqrt(jnp.mean(xf * xf, axis=-1, keepdims=True) + EPS)
    return (y * g.astype(jnp.float32)).astype(x.dtype)


def rope_tables(positions, rot_dim):
    inv_freq = ROPE_THETA ** (-jnp.arange(0, rot_dim, 2, dtype=jnp.float32) / rot_dim)
    ang = positions.astype(jnp.float32)[..., None] * inv_freq
    return jnp.cos(ang)[:, :, None, :], jnp.sin(ang)[:, :, None, :]


def partial_rope(x, tables):
    cos, sin = tables
    half = cos.shape[-1]
    xf = x.astype(jnp.float32)
    x1, x2, rest = xf[..., :half], xf[..., half:2 * half], xf[..., 2 * half:]
    return jnp.concatenate([x1 * cos - x2 * sin, x2 * cos + x1 * sin, rest], axis=-1).astype(x.dtype)


def chunk_gmlp(h, w_in, v_gain, w_s, b_s, w_out):
    B, S, _ = h.shape
    nc = S // GM_CHUNK
    u, v = jnp.split(jax.nn.gelu(h @ w_in), 2, axis=-1)
    v = rmsnorm(v, v_gain).reshape(B, nc, GM_CHUNK, GM_GROUPS, GM_GROUP_DIM)
    causal = jnp.tril(jnp.ones((GM_CHUNK, GM_CHUNK), dtype=w_s.dtype))
    sv = jnp.einsum('gts,bcsgd->bctgd', w_s * causal, v) + b_s.T[:, :, None]
    return (u * sv.reshape(B, S, GM_WIDTH)) @ w_out


def dsa_attention(h, rope_q, rope_i, w_in, w_out):
    B, S, _ = h.shape
    k_top = min(TOPK_MAX, S // 4)
    split_at = [int(c) for c in np.cumsum(B_IN_SIZES)[:-1]]
    q, k, v, qi, ki, wi = jnp.split(h @ w_in, split_at, axis=-1)
    q = partial_rope(q.reshape(B, S, N_HEADS, HEAD_DIM), rope_q)
    k = partial_rope(k.reshape(B, S, N_KV_HEADS, HEAD_DIM), rope_q)
    v = v.reshape(B, S, N_KV_HEADS, HEAD_DIM)
    qi = partial_rope(qi.reshape(B, S, IDX_HEADS, IDX_DIM), rope_i)
    ki_f = partial_rope(ki.reshape(B, S, 1, IDX_DIM), rope_i)[:, :, 0].astype(jnp.float32)
    wi = wi.astype(jnp.float32) * (IDX_HEADS ** -0.5 * IDX_DIM ** -0.5)
    nb = S // Q_BLOCK
    scale = HEAD_DIM ** -0.5
    key_pos = jnp.arange(S)

    def to_blocks(a):
        return jnp.moveaxis(a.reshape(B, nb, Q_BLOCK, *a.shape[2:]), 1, 0)

    def one_block(args):
        qb, qib, wib, blk = args
        t = blk * Q_BLOCK + jnp.arange(Q_BLOCK)
        rel = jax.nn.relu(jnp.einsum('bthd,bsd->bths', qib.astype(jnp.float32), ki_f))
        score = jnp.einsum('bth,bths->bts', wib, rel)
        score = jnp.where((key_pos[None, :] <= t[:, None])[None], score, -jnp.inf)
        _, sel = lax.top_k(score, k_top)
        valid = sel <= t[None, :, None]
        k_sel = jax.vmap(lambda kk, ii: kk[ii])(k, sel)
        v_sel = jax.vmap(lambda vv, ii: vv[ii])(v, sel)
        qg = qb.reshape(B, Q_BLOCK, N_KV_HEADS, N_HEADS // N_KV_HEADS, HEAD_DIM)
        logits = jnp.einsum('btgrd,btkgd->btgrk', qg, k_sel).astype(jnp.float32) * scale
        logits = jnp.where(valid[:, :, None, None, :], logits, -jnp.inf)
        prob = jax.nn.softmax(logits, axis=-1).astype(v.dtype)
        o = jnp.einsum('btgrk,btkgd->btgrd', prob, v_sel)
        return o.reshape(B, Q_BLOCK, N_HEADS * HEAD_DIM)

    out = lax.map(one_block, (to_blocks(q), to_blocks(qi), to_blocks(wi), jnp.arange(nb)))
    out = jnp.moveaxis(out, 0, 1).reshape(B, S, N_HEADS * HEAD_DIM)
    return out @ w_out


def hier_moe(h, w_group, b_group, w_expert, b_expert, w1, w3, w2):
    B, S, D = h.shape
    T = B * S
    xs = h.reshape(T, D)
    g_logits = (xs @ w_group).astype(jnp.float32) + b_group
    g_sel = jnp.argmax(g_logits, axis=-1)
    p_group = jnp.take_along_axis(jax.nn.softmax(g_logits, axis=-1), g_sel[:, None], axis=1)[:, 0]
    e_logits = ((xs @ w_expert).astype(jnp.float32) + b_expert).reshape(T, N_EXPERT_GROUPS, EXPERTS_PER_GROUP)
    local = jnp.take_along_axis(e_logits, g_sel[:, None, None], axis=1)[:, 0]
    top_p, top_i = lax.top_k(jax.nn.softmax(local, axis=-1), EXPERT_TOP_K)
    gates = p_group[:, None] * top_p / jnp.sum(top_p, axis=-1, keepdims=True)
    e_flat = (g_sel[:, None] * EXPERTS_PER_GROUP + top_i).reshape(-1)
    g_flat = gates.reshape(-1)
    tok_flat = jnp.repeat(jnp.arange(T), EXPERT_TOP_K)
    A = T * EXPERT_TOP_K
    order = jnp.argsort(e_flat)
    e_sorted, tok_sorted, g_sorted = e_flat[order], tok_flat[order], g_flat[order]
    counts = jnp.bincount(e_flat, length=N_EXPERTS)
    padded = (counts + MOE_BLOCK - 1) // MOE_BLOCK * MOE_BLOCK
    padded_end = jnp.cumsum(padded)
    padded_start = padded_end - padded
    start = jnp.cumsum(counts) - counts
    dest = padded_start[e_sorted] + jnp.arange(A) - start[e_sorted]
    R = A + N_EXPERTS * MOE_BLOCK
    n_blocks = R // MOE_BLOCK
    buf_tok = jnp.full((R,), T, dtype=tok_sorted.dtype).at[dest].set(tok_sorted)
    x_pad = jnp.concatenate([xs, jnp.zeros((1, D), xs.dtype)], axis=0)
    buf_x = x_pad[buf_tok].reshape(n_blocks, MOE_BLOCK, D)
    block_expert = jnp.minimum(
        jnp.searchsorted(padded_end, jnp.arange(n_blocks) * MOE_BLOCK, side='right'), N_EXPERTS - 1)

    def expert_block(args):
        xb, e = args
        return (jax.nn.silu(xb @ w1[e]) * (xb @ w3[e])) @ w2[e]

    buf_y = lax.map(expert_block, (buf_x, block_expert)).reshape(R, D)
    y = buf_y[dest] * g_sorted[:, None].astype(buf_y.dtype)
    return jax.ops.segment_sum(y, tok_sorted, num_segments=T).reshape(B, S, D)


def setup_inputs(seed: int = 0) -> dict:
    key = jax.random.key(seed)
    ks = jax.random.split(key, 24)
    f32 = jnp.float32

    def normal(k, shape, fan_in):
        return jax.random.normal(k, shape, f32) * fan_in ** -0.5

    def gain(k, shape, s=0.05):
        return 1.0 + s * jax.random.normal(k, shape, f32)

    positions = (jax.random.randint(ks[2], (BATCH, 1), 0, 1024, dtype=jnp.int32)
                 + jnp.arange(SEQ, dtype=jnp.int32)[None, :])
    return {
        'x': jax.random.normal(ks[0], (BATCH, SEQ, D_MODEL), f32),
        'p': jax.random.normal(ks[1], (DEPTH, BATCH, SEQ, PLE_DIM), f32),
        'positions': positions,
        'mix_norm': gain(ks[3], (DEPTH, D_MODEL)),
        'a_w_in': normal(ks[4], (N_A_LAYERS, D_MODEL, 2 * GM_WIDTH), D_MODEL),
        'a_v_gain': gain(ks[5], (N_A_LAYERS, GM_WIDTH)),
        'a_w_s': normal(ks[6], (N_A_LAYERS, GM_GROUPS, GM_CHUNK, GM_CHUNK), GM_CHUNK),
        'a_b_s': gain(ks[7], (N_A_LAYERS, GM_GROUPS, GM_CHUNK), 0.1),
        'a_w_out': normal(ks[8], (N_A_LAYERS, GM_WIDTH, D_MODEL), GM_WIDTH),
        'b_w_in': normal(ks[9], (N_B_LAYERS, D_MODEL, B_IN_DIM), D_MODEL),
        'b_w_out': normal(ks[10], (N_B_LAYERS, N_HEADS * HEAD_DIM, D_MODEL), N_HEADS * HEAD_DIM),
        'ffn_norm': gain(ks[11], (DEPTH, D_MODEL)),
        'moe_w_group': normal(ks[12], (DEPTH, D_MODEL, N_EXPERT_GROUPS), D_MODEL),
        'moe_b_group': 0.01 * jax.random.normal(ks[13], (DEPTH, N_EXPERT_GROUPS), f32),
        'moe_w_expert': normal(ks[14], (DEPTH, D_MODEL, N_EXPERTS), D_MODEL),
        'moe_b_expert': 0.01 * jax.random.normal(ks[15], (DEPTH, N_EXPERTS), f32),
        'moe_w1': normal(ks[16], (DEPTH, N_EXPERTS, D_MODEL, D_EXPERT), D_MODEL),
        'moe_w3': normal(ks[17], (DEPTH, N_EXPERTS, D_MODEL, D_EXPERT), D_MODEL),
        'moe_w2': normal(ks[18], (DEPTH, N_EXPERTS, D_EXPERT, D_MODEL), D_EXPERT),
        'ple_norm': gain(ks[19], (DEPTH, D_MODEL)),
        'ple_w_gate': normal(ks[20], (DEPTH, D_MODEL, D_MODEL), D_MODEL),
        'ple_w_proj': normal(ks[21], (DEPTH, PLE_DIM, D_MODEL), PLE_DIM),
        'final_norm': gain(ks[22], (D_MODEL,)),
    }


def reference(x, p, positions, mix_norm, a_w_in, a_v_gain, a_w_s, a_b_s, a_w_out, b_w_in, b_w_out,
              ffn_norm, moe_w_group, moe_b_group, moe_w_expert, moe_b_expert, moe_w1, moe_w3, moe_w2,
              ple_norm, ple_w_gate, ple_w_proj, final_norm):
    rope_q = rope_tables(positions, ROT_DIM)
    rope_i = rope_tables(positions, IDX_ROT_DIM)
    h = x
    for i in range(DEPTH):
        hn = rmsnorm(h, mix_norm[i])
        j = i // N_MIXERS
        if i % N_MIXERS == 0:
            h = h + chunk_gmlp(hn, a_w_in[j], a_v_gain[j], a_w_s[j], a_b_s[j], a_w_out[j])
        else:
            h = h + dsa_attention(hn, rope_q, rope_i, b_w_in[j], b_w_out[j])
        h = h + hier_moe(rmsnorm(h, ffn_norm[i]), moe_w_group[i], moe_b_group[i], moe_w_expert[i],
                         moe_b_expert[i], moe_w1[i], moe_w3[i], moe_w2[i])
        gate = jax.nn.sigmoid(rmsnorm(h, ple_norm[i]) @ ple_w_gate[i])
        h = h + gate * (p[i] @ ple_w_proj[i])
    return rmsnorm(h, final_norm)
```

```python
import functools

import numpy as np
import jax
import jax.numpy as jnp
from jax import lax
from jax.experimental import pallas as pl
from jax.experimental.pallas import tpu as pltpu

F32 = jnp.float32
BF16 = jnp.bfloat16
I32 = jnp.int32

D_MODEL = 1024
EPS = 1e-6
ROPE_THETA = 500000.0
GM_CHUNK = 128
GM_WIDTH = 2 * D_MODEL
GM_GROUPS = 8
GM_GROUP_DIM = GM_WIDTH // GM_GROUPS
N_HEADS = 8
HEAD_DIM = D_MODEL // N_HEADS
N_KV_HEADS = 2
HEADS_PER_KV = N_HEADS // N_KV_HEADS
KV_DIM = N_KV_HEADS * HEAD_DIM
ROT_DIM = HEAD_DIM // 4
IDX_HEADS = 8
IDX_DIM = 64
IDX_ROT_DIM = IDX_DIM // 4
TOPK_MAX = 256
N_EXPERT_GROUPS = 4
EXPERTS_PER_GROUP = 8
N_EXPERTS = N_EXPERT_GROUPS * EXPERTS_PER_GROUP
D_EXPERT = D_MODEL // 2
PLE_DIM = 256

LANES = 128
V7X_VMEM_LIMIT = 56 * 1024 * 1024

TM_MIX = 256
TM_RANK = 512
TM_DISPATCH = 512
TM_PLE = 256
MOE_ROWS = 256
Q_BLOCK = 128
K_TILE = 512
ATT_GROUPS = 4
NEG_BIG = -1e30

R_E0, R_E1, R_G0, R_G1 = 0, 1, 2, 3
R_EXP0 = N_EXPERT_GROUPS


def _cparams(*sem):
    return pltpu.CompilerParams(dimension_semantics=sem, vmem_limit_bytes=V7X_VMEM_LIMIT)


def _const_spec(shape):
    nd = len(shape)
    return pl.BlockSpec(shape, lambda *_: (0,) * nd, pipeline_mode=pl.Buffered(1))


def _rms(x, g):
    return x * lax.rsqrt(jnp.mean(x * x, axis=-1, keepdims=True) + EPS) * g


def _gelu(x):
    return jax.nn.gelu(x)


def _dot(a, b):
    return jnp.dot(a, b, preferred_element_type=F32)


def _dot_nt(a, b):
    return lax.dot_general(a, b, (((1,), (1,)), ((), ())), preferred_element_type=F32)


def _split_bf16(a):
    hi = a.astype(BF16)
    lo = (a - hi.astype(F32)).astype(BF16)
    return hi, lo


def _dot_3pass(a, b):
    ah, al = _split_bf16(a)
    bh, bl = _split_bf16(b)
    return _dot(ah, bh) + (_dot(al, bh) + _dot(ah, bl))


def _route(xn, wr_ref, br_ref):
    tm = xn.shape[0]
    logits = _dot_3pass(xn, wr_ref[...]) + br_ref[...]
    lane = lax.broadcasted_iota(I32, (tm, LANES), 1)
    neg_inf = jnp.float32(-jnp.inf)

    gmask = lane < N_EXPERT_GROUPS
    gl = jnp.where(gmask, logits, neg_inf)
    gmax = jnp.max(gl, axis=-1, keepdims=True)
    g_sel = jnp.min(jnp.where(gl == gmax, lane, LANES), axis=-1, keepdims=True)
    p_group = 1.0 / jnp.sum(jnp.where(gmask, jnp.exp(gl - gmax), 0.0), axis=-1, keepdims=True)

    lo = R_EXP0 + EXPERTS_PER_GROUP * g_sel
    lmask = (lane >= lo) & (lane < lo + EXPERTS_PER_GROUP)
    ll = jnp.where(lmask, logits, neg_inf)
    lmax = jnp.max(ll, axis=-1, keepdims=True)
    le = jnp.where(lmask, jnp.exp(ll - lmax), 0.0)
    probs = jnp.where(lmask, le / jnp.sum(le, axis=-1, keepdims=True), -1.0)
    p1 = jnp.max(probs, axis=-1, keepdims=True)
    i1 = jnp.min(jnp.where(probs == p1, lane, LANES), axis=-1, keepdims=True)
    probs2 = jnp.where(lane == i1, -1.0, probs)
    p2 = jnp.max(probs2, axis=-1, keepdims=True)
    i2 = jnp.min(jnp.where(probs2 == p2, lane, LANES), axis=-1, keepdims=True)
    denom = p1 + p2
    g1 = p_group * p1 / denom
    g2 = p_group * p2 / denom
    e1 = (i1 - R_EXP0).astype(F32)
    e2 = (i2 - R_EXP0).astype(F32)
    out = jnp.where(lane == R_E0, e1, 0.0)
    out = jnp.where(lane == R_E1, e2, out)
    out = jnp.where(lane == R_G0, g1, out)
    out = jnp.where(lane == R_G1, g2, out)
    return out


def _router_weights(w_group, b_group, w_expert, b_expert):
    wr = jnp.zeros((D_MODEL, LANES), F32)
    wr = wr.at[:, :N_EXPERT_GROUPS].set(w_group).at[:, R_EXP0:R_EXP0 + N_EXPERTS].set(w_expert)
    br = jnp.zeros((1, LANES), F32)
    br = br.at[0, :N_EXPERT_GROUPS].set(b_group).at[0, R_EXP0:R_EXP0 + N_EXPERTS].set(b_expert)
    return wr, br


def _gmlp_kernel(x_ref, mixg_ref, win_ref, vgain_ref, ws_ref, bst_ref, wout_ref, ffng_ref, wr_ref,
                 br_ref, h_ref, xn_ref, route_ref, xn_s, v_s, prod_s):
    tm = x_ref.shape[0]
    xn_s[...] = _rms(x_ref[...], mixg_ref[...]).astype(BF16)

    nt = 512
    ssq = jnp.zeros((tm, 1), F32)
    for j in range(GM_WIDTH // nt):
        z = _gelu(_dot(xn_s[...], win_ref[:, GM_WIDTH + j * nt:GM_WIDTH + (j + 1) * nt]))
        v_s[:, j * nt:(j + 1) * nt] = z
        ssq = ssq + jnp.sum(z * z, axis=-1, keepdims=True)
    rinv = lax.rsqrt(ssq * (1.0 / GM_WIDTH) + EPS)

    row = lax.broadcasted_iota(I32, (GM_CHUNK, GM_CHUNK), 0)
    col = lax.broadcasted_iota(I32, (GM_CHUNK, GM_CHUNK), 1)
    tril = row >= col
    gd = GM_GROUP_DIM
    for g in range(GM_GROUPS):
        wm = jnp.where(tril, ws_ref[g], 0.0).astype(BF16)
        vg = (v_s[:, g * gd:(g + 1) * gd] * rinv * vgain_ref[:, g * gd:(g + 1) * gd]).astype(BF16)
        u = _gelu(_dot(xn_s[...], win_ref[:, g * gd:(g + 1) * gd]))
        bias = bst_ref[:, g:g + 1]
        for c in range(tm // GM_CHUNK):
            rows = slice(c * GM_CHUNK, (c + 1) * GM_CHUNK)
            sv = _dot(wm, vg[rows, :]) + bias
            prod_s[rows, g * gd:(g + 1) * gd] = (u[rows, :] * sv).astype(BF16)

    h = x_ref[...] + _dot(prod_s[...], wout_ref[...])
    h_ref[...] = h
    xn = _rms(h, ffng_ref[...])
    xn_ref[...] = xn
    route_ref[...] = _route(xn, wr_ref, br_ref)


def _gmlp_layer(x, mix_g, w_in, v_gain, w_s, b_s, w_out, ffn_g, wr, br):
    t = x.shape[0]
    tm = TM_MIX
    row_spec = pl.BlockSpec((tm, D_MODEL), lambda i: (i, 0))
    return pl.pallas_call(
        _gmlp_kernel,
        grid=(t // tm,),
        in_specs=[
            row_spec,
            _const_spec((1, D_MODEL)),
            _const_spec((D_MODEL, 2 * GM_WIDTH)),
            _const_spec((1, GM_WIDTH)),
            _const_spec((GM_GROUPS, GM_CHUNK, GM_CHUNK)),
            _const_spec((GM_CHUNK, GM_GROUPS)),
            _const_spec((GM_WIDTH, D_MODEL)),
            _const_spec((1, D_MODEL)),
            _const_spec((D_MODEL, LANES)),
            _const_spec((1, LANES)),
        ],
        out_specs=[row_spec, row_spec, pl.BlockSpec((tm, LANES), lambda i: (i, 0))],
        out_shape=[jax.ShapeDtypeStruct((t, D_MODEL), F32), jax.ShapeDtypeStruct((t, D_MODEL), F32),
                   jax.ShapeDtypeStruct((t, LANES), F32)],
        scratch_shapes=[pltpu.VMEM((tm, D_MODEL), BF16), pltpu.VMEM((tm, GM_WIDTH), F32),
                        pltpu.VMEM((tm, GM_WIDTH), BF16)],
        compiler_params=_cparams("parallel"),
        name="gmlp_mixer",
    )(x, mix_g.reshape(1, -1), w_in.astype(BF16), v_gain.reshape(1, -1), w_s, b_s.T,
      w_out.astype(BF16), ffn_g.reshape(1, -1), wr, br)


def _rank_kernel(route_ref, rank_ref, cnt_ref, carry_s):
    tm = route_ref.shape[0]

    @pl.when(pl.program_id(0) == 0)
    def _():
        carry_s[...] = jnp.zeros_like(carry_s)

    r = route_ref[...]
    lane = lax.broadcasted_iota(I32, (tm, LANES), 1)
    lanef = lane.astype(F32)
    oh0 = (lanef == r[:, R_E0:R_E0 + 1]).astype(F32)
    oh1 = (lanef == r[:, R_E1:R_E1 + 1]).astype(F32)
    both = oh0 + oh1
    row = lax.broadcasted_iota(I32, (tm, tm), 0)
    col = lax.broadcasted_iota(I32, (tm, tm), 1)
    strict_lower = (row > col).astype(BF16)
    base = carry_s[0:1, :] + _dot(strict_lower, both.astype(BF16))
    rk0 = jnp.sum(oh0 * base, axis=-1, keepdims=True)
    rk1 = jnp.sum(oh1 * (base + oh0), axis=-1, keepdims=True)
    rank_ref[...] = jnp.where(lane == 0, rk0, jnp.where(lane == 1, rk1, 0.0))
    carry_s[0:1, :] = carry_s[0:1, :] + jnp.sum(both, axis=0, keepdims=True)
    cnt_ref[...] = carry_s[...]


def _moe_rank(route):
    t = route.shape[0]
    tm = min(TM_RANK, t)
    return pl.pallas_call(
        _rank_kernel,
        grid=(t // tm,),
        in_specs=[pl.BlockSpec((tm, LANES), lambda i: (i, 0))],
        out_specs=[pl.BlockSpec((tm, LANES), lambda i: (i, 0)), pl.BlockSpec((8, LANES), lambda i: (0, 0))],
        out_shape=[jax.ShapeDtypeStruct((t, LANES), F32), jax.ShapeDtypeStruct((8, LANES), F32)],
        scratch_shapes=[pltpu.VMEM((8, LANES), F32)],
        compiler_params=_cparams("arbitrary"),
        name="moe_rank",
    )(route)


def _row_copy(src_ref, dst_ref, sem, s, d):
    return pltpu.make_async_copy(src_ref.at[pl.ds(s, 1)], dst_ref.at[pl.ds(d, 1)], sem)


def _dispatch_kernel(dest_ref, xn_ref, bufx_ref, sem):
    tm = xn_ref.shape[0]

    def issue(t, c):
        for k in range(2):
            _row_copy(xn_ref, bufx_ref, sem, t, dest_ref[2 * t + k]).start()
        return c

    lax.fori_loop(0, tm, issue, 0)

    def drain(t, c):
        for k in range(2):
            _row_copy(xn_ref, bufx_ref, sem, 0, 0).wait()
        return c

    lax.fori_loop(0, tm, drain, 0)


def _moe_dispatch(xn, dest_flat, n_rows):
    t = xn.shape[0]
    tm = min(TM_DISPATCH, t)
    return pl.pallas_call(
        _dispatch_kernel,
        grid=(t // tm,),
        in_specs=[pl.BlockSpec((2 * tm,), lambda i: (i,), memory_space=pltpu.SMEM),
                  pl.BlockSpec((tm, D_MODEL), lambda i: (i, 0))],
        out_specs=pl.BlockSpec(memory_space=pl.ANY),
        out_shape=jax.ShapeDtypeStruct((n_rows, D_MODEL), F32),
        scratch_shapes=[pltpu.SemaphoreType.DMA(())],
        compiler_params=_cparams("arbitrary"),
        name="moe_dispatch",
    )(dest_flat, xn)


def _expert_kernel(ie_ref, ib_ref, ilo_ref, ihi_ref, ifirst_ref, n_ref,
                   x_ref, w1_ref, w3_ref, w2_ref, y_ref, w1_s, w3_s, w2_s):
    w = pl.program_id(0)
    live = w < n_ref[0]
    e = ie_ref[w]
    prev = ie_ref[jnp.maximum(w - 1, 0)]

    @pl.when(live & ((w == 0) | (e != prev)))
    def _():
        w1_s[...] = w1_ref[0].astype(BF16)
        w3_s[...] = w3_ref[0].astype(BF16)
        w2_s[...] = w2_ref[0].astype(BF16)

    def ffn():
        rows = lax.broadcasted_iota(I32, (x_ref.shape[0], 1), 0)
        mine = (rows >= ilo_ref[w]) & (rows < ihi_ref[w])
        x = jnp.where(mine, x_ref[...], 0.0).astype(BF16)
        a = _dot(x, w1_s[...])
        b = _dot(x, w3_s[...])
        mid = (a * jax.nn.sigmoid(a) * b).astype(BF16)
        return _dot(mid, w2_s[...])

    @pl.when(live & (ifirst_ref[w] == 1))
    def _():
        y_ref[...] = ffn()

    @pl.when(live & (ifirst_ref[w] == 0))
    def _():
        y_ref[...] += ffn()


def _moe_experts(bufx, items, w1, w3, w2):
    n_rows = bufx.shape[0]
    br = MOE_ROWS
    n_items_max = items[0].shape[0]

    def row_map(w, ie, ib, ilo, ihi, ifirst, n):
        return (ib[w], 0)

    def w_map(w, ie, ib, ilo, ihi, ifirst, n):
        return (ie[w], 0, 0)

    grid_spec = pltpu.PrefetchScalarGridSpec(
        num_scalar_prefetch=6,
        grid=(n_items_max,),
        in_specs=[pl.BlockSpec((br, D_MODEL), row_map),
                  pl.BlockSpec((1, D_MODEL, D_EXPERT), w_map),
                  pl.BlockSpec((1, D_MODEL, D_EXPERT), w_map),
                  pl.BlockSpec((1, D_EXPERT, D_MODEL), w_map)],
        out_specs=pl.BlockSpec((br, D_MODEL), row_map),
        scratch_shapes=[pltpu.VMEM((D_MODEL, D_EXPERT), BF16), pltpu.VMEM((D_MODEL, D_EXPERT), BF16),
                        pltpu.VMEM((D_EXPERT, D_MODEL), BF16)],
    )
    return pl.pallas_call(
        _expert_kernel,
        grid_spec=grid_spec,
        out_shape=jax.ShapeDtypeStruct((n_rows, D_MODEL), F32),
        compiler_params=_cparams("arbitrary"),
        name="moe_experts",
    )(*items, bufx, w1, w3, w2)


def _moe_plan(route, rank, counts):
    br = MOE_ROWS
    t = route.shape[0]
    nb = 2 * t // br
    n_items_max = nb + N_EXPERTS - 1
    e = route[:, :2].astype(I32)
    rk = rank[:, :2].astype(I32)
    cnt = counts[0, :N_EXPERTS].astype(I32)
    end = jnp.cumsum(cnt)
    start = end - cnt
    dest = (start[e] + rk).reshape(-1)

    first_blk = start // br
    n_blk = jnp.where(cnt > 0, (end - 1) // br - first_blk + 1, 0)
    iend = jnp.cumsum(n_blk)
    istart = iend - n_blk
    n_items = iend[-1:]
    w = jnp.minimum(jnp.arange(n_items_max, dtype=I32), n_items[0] - 1)
    ie = jnp.minimum(jnp.sum((iend[None, :] <= w[:, None]).astype(I32), axis=1), N_EXPERTS - 1)
    ib = first_blk[ie] + (w - istart[ie])
    ilo = jnp.maximum(start[ie], ib * br) - ib * br
    ihi = jnp.minimum(end[ie], (ib + 1) * br) - ib * br
    ifirst = jnp.concatenate([jnp.ones((1,), I32), (ib[1:] != ib[:-1]).astype(I32)])
    items = tuple(a.astype(I32) for a in (ie, ib, ilo, ihi, ifirst, n_items))
    return dest, items


def _moe(xn, route, w1, w3, w2):
    rank, counts = _moe_rank(route)
    dest, items = _moe_plan(route, rank, counts)
    bufx = _moe_dispatch(xn, dest, 2 * xn.shape[0])
    bufy = _moe_experts(bufx, items, w1, w3, w2)
    return bufy, dest


def _ple_kernel(dest_ref, h_ref, route_ref, p_ref, bufy_ref, pleg_ref, wg_ref, wp_ref, fing_ref,
                out_ref, y_s, sem, *, final):
    tm = h_ref.shape[0]

    def issue(t, c):
        for k in range(2):
            _row_copy(bufy_ref, y_s.at[k], sem, dest_ref[2 * t + k], t).start()
        return c

    lax.fori_loop(0, tm, issue, 0)
    pp = _dot(p_ref[...].astype(BF16), wp_ref[...])

    def drain(t, c):
        for k in range(2):
            _row_copy(bufy_ref, y_s.at[k], sem, 0, 0).wait()
        return c

    lax.fori_loop(0, tm, drain, 0)

    r = route_ref[...]
    h = h_ref[...] + (y_s[0] * r[:, R_G0:R_G0 + 1] + y_s[1] * r[:, R_G1:R_G1 + 1])
    gate = jax.nn.sigmoid(_dot(_rms(h, pleg_ref[...]).astype(BF16), wg_ref[...]))
    h = h + gate * pp
    out_ref[...] = _rms(h, fing_ref[...]) if final else h


def _moe_combine_ple(h, route, bufy, dest, p, ple_g, w_gate, w_proj, final_g, final):
    t = h.shape[0]
    tm = min(TM_PLE, t)
    row_spec = pl.BlockSpec((tm, D_MODEL), lambda i: (i, 0))
    return pl.pallas_call(
        functools.partial(_ple_kernel, final=final),
        grid=(t // tm,),
        in_specs=[pl.BlockSpec((2 * tm,), lambda i: (i,), memory_space=pltpu.SMEM),
                  row_spec,
                  pl.BlockSpec((tm, LANES), lambda i: (i, 0)),
                  pl.BlockSpec((tm, PLE_DIM), lambda i: (i, 0)),
                  pl.BlockSpec(memory_space=pl.ANY),
                  _const_spec((1, D_MODEL)),
                  _const_spec((D_MODEL, D_MODEL)),
                  _const_spec((PLE_DIM, D_MODEL)),
                  _const_spec((1, D_MODEL))],
        out_specs=row_spec,
        out_shape=jax.ShapeDtypeStruct((t, D_MODEL), F32),
        scratch_shapes=[pltpu.VMEM((2, tm, D_MODEL), F32), pltpu.SemaphoreType.DMA(())],
        compiler_params=_cparams("arbitrary"),
        name="moe_combine_ple",
    )(dest, h, route, p, bufy, ple_g.reshape(1, -1), w_gate.astype(BF16), w_proj.astype(BF16),
      final_g.reshape(1, -1))


_P_Q = 0
_P_K = _P_Q + N_HEADS * HEAD_DIM
_P_V = _P_K + KV_DIM
_P_QI = _P_V + KV_DIM
_P_KI = _P_QI + IDX_HEADS * LANES
_P_WI = _P_KI + LANES
_P_END = _P_WI + LANES


def _rope_tables(pos, inv_ref, half):
    tm = pos.shape[0]
    lane = lax.broadcasted_iota(I32, (tm, LANES), 1)
    ang = pos * inv_ref[...]
    rot = lane < 2 * half
    cos = jnp.where(rot, jnp.cos(ang), 1.0)
    sin = jnp.sin(ang)
    s_lo = jnp.where(lane < half, -sin, 0.0)
    s_hi = jnp.where(rot & (lane >= half), sin, 0.0)
    return cos, s_lo, s_hi


def _rope(x, tables, half):
    cos, s_lo, s_hi = tables
    return x * cos + pltpu.roll(x, LANES - half, 1) * s_lo + pltpu.roll(x, half, 1) * s_hi


def _dsa_in_kernel(h_ref, g_ref, pos_ref, w_ref, invq_ref, invi_ref,
                   q_ref, k_ref, v_ref, qi_ref, ki_ref, wi_ref, hn_s):
    hn_s[...] = _rms(h_ref[...], g_ref[...]).astype(BF16)
    pos = pos_ref[...].astype(F32)
    tq = _rope_tables(pos, invq_ref, ROT_DIM // 2)
    ti = _rope_tables(pos, invi_ref, IDX_ROT_DIM // 2)

    def proj(c0, width):
        return _dot(hn_s[...], w_ref[:, c0:c0 + width])

    for hd in range(N_HEADS):
        c = hd * HEAD_DIM
        q_ref[:, c:c + HEAD_DIM] = _rope(proj(_P_Q + c, HEAD_DIM), tq, ROT_DIM // 2).astype(BF16)
    for hd in range(N_KV_HEADS):
        c = hd * HEAD_DIM
        k_ref[:, c:c + HEAD_DIM] = _rope(proj(_P_K + c, HEAD_DIM), tq, ROT_DIM // 2).astype(BF16)
    v_ref[...] = proj(_P_V, KV_DIM).astype(BF16)
    for hd in range(IDX_HEADS):
        c = hd * LANES
        qi_ref[:, c:c + LANES] = _rope(proj(_P_QI + c, LANES), ti, IDX_ROT_DIM // 2).astype(BF16)
    ki_ref[...] = _rope(proj(_P_KI, LANES), ti, IDX_ROT_DIM // 2).astype(BF16)
    wi_ref[...] = proj(_P_WI, LANES) * (IDX_HEADS ** -0.5 * IDX_DIM ** -0.5)


def _inv_freq_lanes(rot_dim):
    half = rot_dim // 2
    inv = ROPE_THETA ** (-jnp.arange(0, rot_dim, 2, dtype=F32) / rot_dim)
    lanes = jnp.zeros((LANES,), F32).at[:rot_dim].set(jnp.concatenate([inv, inv]))
    del half
    return lanes.reshape(1, LANES)


def _dsa_in_weights(w_in):
    d = w_in.shape[0]
    c_q, c_k, c_v = N_HEADS * HEAD_DIM, KV_DIM, KV_DIM
    c_qi, c_ki = IDX_HEADS * IDX_DIM, IDX_DIM
    o = np.cumsum([0, c_q, c_k, c_v, c_qi, c_ki, IDX_HEADS])
    wq, wk, wv, wqi, wki, wwi = (w_in[:, o[i]:o[i + 1]] for i in range(6))
    wqi = jnp.pad(wqi.reshape(d, IDX_HEADS, IDX_DIM), ((0, 0), (0, 0), (0, LANES - IDX_DIM)))
    wki = jnp.pad(wki, ((0, 0), (0, LANES - IDX_DIM)))
    wwi = jnp.pad(wwi, ((0, 0), (0, LANES - IDX_HEADS)))
    return jnp.concatenate([wq, wk, wv, wqi.reshape(d, IDX_HEADS * LANES), wki, wwi], axis=1).astype(BF16)


def _dsa_in(h, mix_g, positions, w_in):
    t = h.shape[0]
    tm = min(TM_MIX, t)
    spec = lambda w: pl.BlockSpec((tm, w), lambda i: (i, 0))
    widths = (N_HEADS * HEAD_DIM, KV_DIM, KV_DIM, IDX_HEADS * LANES, LANES)
    return pl.pallas_call(
        _dsa_in_kernel,
        grid=(t // tm,),
        in_specs=[spec(D_MODEL), _const_spec((1, D_MODEL)), spec(1), _const_spec((D_MODEL, _P_END)),
                  _const_spec((1, LANES)), _const_spec((1, LANES))],
        out_specs=[spec(w) for w in widths] + [spec(LANES)],
        out_shape=[jax.ShapeDtypeStruct((t, w), BF16) for w in widths]
        + [jax.ShapeDtypeStruct((t, LANES), F32)],
        scratch_shapes=[pltpu.VMEM((tm, D_MODEL), BF16)],
        compiler_params=_cparams("parallel"),
        name="dsa_in_proj",
    )(h, mix_g.reshape(1, -1), positions.reshape(t, 1), _dsa_in_weights(w_in),
      _inv_freq_lanes(ROT_DIM), _inv_freq_lanes(IDX_ROT_DIM))


_INT_MIN = -2 ** 31
_U_NEG_INF = 0x007FFFFF


def _key_to_f32(u):
    bits = jnp.where(u < 0, u ^ _INT_MIN, ~u)
    return lax.bitcast_convert_type(bits, F32)


def _dsa_core_kernel(q_ref, qi_ref, wi_ref, k_ref, v_ref, ki_ref, o_ref,
                     score_s, bias_s, m_s, l_s, acc_s, *, q0, k_top):
    qb = q_ref.shape[0]
    sk = k_ref.shape[0]
    n_kt = sk // K_TILE
    n_ch = sk // LANES
    qpos = (q0 + pl.program_id(1)) * qb + lax.broadcasted_iota(I32, (qb, 1), 0)
    neg_inf = jnp.float32(-jnp.inf)

    wcols = [wi_ref[:, hd:hd + 1] for hd in range(IDX_HEADS)]

    def score_tile(kt, c):
        k0 = pl.multiple_of(kt * K_TILE, K_TILE)
        kit = ki_ref[pl.ds(k0, K_TILE), :]
        acc = jnp.zeros((qb, K_TILE), F32)
        for hd in range(IDX_HEADS):
            rel = jnp.maximum(_dot_nt(qi_ref[:, hd * LANES:(hd + 1) * LANES], kit), 0.0)
            acc = acc + wcols[hd] * rel
        kpos = k0 + lax.broadcasted_iota(I32, (qb, K_TILE), 1)
        score_s[:, pl.ds(k0, K_TILE)] = jnp.where(kpos <= qpos, acc, neg_inf)
        return c

    lax.fori_loop(0, n_kt, score_tile, 0)

    def count_ge(thr):
        acc = jnp.zeros((qb, LANES), F32)
        for c in range(n_ch):
            acc = acc + jnp.where(score_s[:, c * LANES:(c + 1) * LANES] >= thr, 1.0, 0.0)
        return jnp.sum(acc, axis=-1, keepdims=True)

    kf = jnp.float32(k_top)

    def bisect(p, ans):
        bit = lax.shift_left(jnp.int32(1), 31 - p)
        cand = ans | bit
        force = (cand >= 0) & (cand <= _U_NEG_INF)
        ok = (count_ge(_key_to_f32(cand)) >= kf) | force
        return jnp.where(ok, cand, ans)

    ans = lax.fori_loop(0, 32, bisect, jnp.zeros((qb, 1), I32))
    thr = _key_to_f32(jnp.where(ans == _U_NEG_INF, ans + 1, ans))
    thr_up = _key_to_f32(ans + 1)

    c_hi = count_ge(thr_up)
    need0 = kf - c_hi
    rem0 = count_ge(thr) - c_hi
    for c in range(n_ch):
        s = score_s[:, c * LANES:(c + 1) * LANES]
        bias_s[:, c * LANES:(c + 1) * LANES] = jnp.where(s >= thr_up, 0.0, NEG_BIG)

    def active_of(need, rem):
        return (need > 0.0) & (rem > 0.0)

    def bin_cond(carry):
        return carry[2] > 0.0

    def bin_body(carry):
        need, rem, _ = carry
        active = active_of(need, rem)
        best = jnp.full((qb, LANES), neg_inf, F32)
        for c in range(n_ch):
            s = score_s[:, c * LANES:(c + 1) * LANES]
            b = bias_s[:, c * LANES:(c + 1) * LANES]
            best = jnp.maximum(best, jnp.where((s >= thr) & (b != 0.0), s, neg_inf))
        m = jnp.max(best, axis=-1, keepdims=True)
        first = jnp.full((qb, LANES), sk, I32)
        lane = lax.broadcasted_iota(I32, (qb, LANES), 1)
        for c in range(n_ch):
            s = score_s[:, c * LANES:(c + 1) * LANES]
            b = bias_s[:, c * LANES:(c + 1) * LANES]
            hit = (s == m) & (s >= thr) & (b != 0.0)
            first = jnp.minimum(first, jnp.where(hit, lane + c * LANES, sk))
        pick = jnp.min(first, axis=-1, keepdims=True)
        for c in range(n_ch):
            b = bias_s[:, c * LANES:(c + 1) * LANES]
            sel = active & (lane + c * LANES == pick)
            bias_s[:, c * LANES:(c + 1) * LANES] = jnp.where(sel, 0.0, b)
        step = jnp.where(active, 1.0, 0.0)
        need, rem = need - step, rem - step
        go = jnp.max(jnp.where(active_of(need, rem), 1.0, 0.0))
        return need, rem, go

    go0 = jnp.max(jnp.where(active_of(need0, rem0), 1.0, 0.0))
    lax.while_loop(bin_cond, bin_body, (need0, rem0, go0))

    scale = HEAD_DIM ** -0.5
    rows = HEADS_PER_KV * qb
    for g in range(N_KV_HEADS):
        qg = jnp.concatenate(
            [q_ref[:, (g * HEADS_PER_KV + r) * HEAD_DIM:(g * HEADS_PER_KV + r + 1) * HEAD_DIM]
             for r in range(HEADS_PER_KV)], axis=0)
        m_s[...] = jnp.full(m_s.shape, NEG_BIG, F32)
        l_s[...] = jnp.zeros(l_s.shape, F32)
        acc_s[...] = jnp.zeros(acc_s.shape, F32)

        def att_tile(kt, c, qg=qg, g=g):
            k0 = pl.multiple_of(kt * K_TILE, K_TILE)
            kt_ = k_ref[pl.ds(k0, K_TILE), g * HEAD_DIM:(g + 1) * HEAD_DIM]
            vt_ = v_ref[pl.ds(k0, K_TILE), g * HEAD_DIM:(g + 1) * HEAD_DIM]
            bias = bias_s[:, pl.ds(k0, K_TILE)]
            logits = _dot_nt(qg, kt_) * scale + jnp.concatenate([bias] * HEADS_PER_KV, axis=0)
            m_old = m_s[...]
            m_new = jnp.maximum(m_old, jnp.max(logits, axis=-1, keepdims=True))
            alpha = jnp.exp(m_old - m_new)
            pexp = jnp.exp(logits - m_new)
            l_s[...] = alpha * l_s[...] + jnp.sum(pexp, axis=-1, keepdims=True)
            acc_s[...] = alpha * acc_s[...] + _dot(pexp.astype(BF16), vt_)
            m_s[...] = m_new
            return c

        lax.fori_loop(0, n_kt, att_tile, 0)
        og = acc_s[...] / l_s[...]
        for r in range(HEADS_PER_KV):
            c0 = (g * HEADS_PER_KV + r) * HEAD_DIM
            o_ref[:, c0:c0 + HEAD_DIM] = og[r * qb:(r + 1) * qb, :].astype(BF16)
    del rows


def _dsa_core(q, k, v, qi, ki, wi, batch, seq):
    qb = Q_BLOCK
    k_top = min(TOPK_MAX, seq // 4)
    nq = seq // qb
    groups = ATT_GROUPS if nq % ATT_GROUPS == 0 and (seq // ATT_GROUPS) % K_TILE == 0 else 1
    gq = nq // groups
    view = lambda a: a.reshape(batch, seq, a.shape[-1])
    q3, k3, v3, qi3, ki3, wi3 = (view(a) for a in (q, k, v, qi, ki, wi))
    outs = []
    for gi in range(groups):
        sk = (gi + 1) * gq * qb
        q0 = gi * gq
        qspec = lambda w, q0=q0: pl.BlockSpec((None, qb, w), lambda b, j: (b, q0 + j, 0))
        kspec = lambda w, sk=sk: pl.BlockSpec((None, sk, w), lambda b, j: (b, 0, 0))
        rows = HEADS_PER_KV * qb
        outs.append(pl.pallas_call(
            functools.partial(_dsa_core_kernel, q0=q0, k_top=k_top),
            grid=(batch, gq),
            in_specs=[qspec(D_MODEL), qspec(IDX_HEADS * LANES), qspec(LANES),
                      kspec(KV_DIM), kspec(KV_DIM), kspec(LANES)],
            out_specs=pl.BlockSpec((None, qb, D_MODEL), lambda b, j: (b, j, 0)),
            out_shape=jax.ShapeDtypeStruct((batch, gq * qb, D_MODEL), BF16),
            scratch_shapes=[pltpu.VMEM((qb, sk), F32), pltpu.VMEM((qb, sk), F32),
                            pltpu.VMEM((rows, 1), F32), pltpu.VMEM((rows, 1), F32),
                            pltpu.VMEM((rows, HEAD_DIM), F32)],
            compiler_params=_cparams("parallel", "arbitrary"),
            name=f"dsa_core_{gi}",
        )(q3, qi3, wi3, k3, v3, ki3))
    out = outs[0] if groups == 1 else jnp.concatenate(outs, axis=1)
    return out.reshape(batch * seq, D_MODEL)


def _dsa_out_kernel(h_ref, a_ref, wo_ref, ffng_ref, wr_ref, br_ref, h_out_ref, xn_ref, route_ref):
    h = h_ref[...] + _dot(a_ref[...], wo_ref[...])
    h_out_ref[...] = h
    xn = _rms(h, ffng_ref[...])
    xn_ref[...] = xn
    route_ref[...] = _route(xn, wr_ref, br_ref)


def _dsa_out(h, att, w_out, ffn_g, wr, br):
    t = h.shape[0]
    tm = min(TM_MIX, t)
    row_spec = pl.BlockSpec((tm, D_MODEL), lambda i: (i, 0))
    return pl.pallas_call(
        _dsa_out_kernel,
        grid=(t // tm,),
        in_specs=[row_spec, row_spec, _const_spec((D_MODEL, D_MODEL)), _const_spec((1, D_MODEL)),
                  _const_spec((D_MODEL, LANES)), _const_spec((1, LANES))],
        out_specs=[row_spec, row_spec, pl.BlockSpec((tm, LANES), lambda i: (i, 0))],
        out_shape=[jax.ShapeDtypeStruct((t, D_MODEL), F32), jax.ShapeDtypeStruct((t, D_MODEL), F32),
                   jax.ShapeDtypeStruct((t, LANES), F32)],
        compiler_params=_cparams("parallel"),
        name="dsa_out_proj",
    )(h, att, w_out.astype(BF16), ffn_g.reshape(1, -1), wr, br)


def kernel(x, p, positions, mix_norm, a_w_in, a_v_gain, a_w_s, a_b_s, a_w_out, b_w_in, b_w_out,
           ffn_norm, moe_w_group, moe_b_group, moe_w_expert, moe_b_expert, moe_w1, moe_w3, moe_w2,
           ple_norm, ple_w_gate, ple_w_proj, final_norm):
    batch, seq, d = x.shape
    depth = mix_norm.shape[0]
    t = batch * seq
    h = x.reshape(t, d)
    for i in range(depth):
        j = i // 2
        wr, br = _router_weights(moe_w_group[i], moe_b_group[i], moe_w_expert[i], moe_b_expert[i])
        if i % 2 == 0:
            h, xn, route = _gmlp_layer(h, mix_norm[i], a_w_in[j], a_v_gain[j], a_w_s[j], a_b_s[j],
                                       a_w_out[j], ffn_norm[i], wr, br)
        else:
            q, k, v, qi, ki, wi = _dsa_in(h, mix_norm[i], positions, b_w_in[j])
            att = _dsa_core(q, k, v, qi, ki, wi, batch, seq)
            h, xn, route = _dsa_out(h, att, b_w_out[j], ffn_norm[i], wr, br)
        bufy, dest = _moe(xn, route, moe_w1[i], moe_w3[i], moe_w2[i])
        h = _moe_combine_ple(h, route, bufy, dest, p[i].reshape(t, -1), ple_norm[i], ple_w_gate[i],
                             ple_w_proj[i], final_norm, final=(i == depth - 1))
    return h.reshape(batch, seq, d)
```

```python
import functools

import numpy as np
import jax
import jax.numpy as jnp
from jax import lax
from jax.experimental import pallas as pl
from jax.experimental.pallas import tpu as pltpu

F32 = jnp.float32
BF16 = jnp.bfloat16
I32 = jnp.int32

D_MODEL = 1024
EPS = 1e-6
ROPE_THETA = 500000.0
GM_CHUNK = 128
GM_WIDTH = 2 * D_MODEL
GM_GROUPS = 8
GM_GROUP_DIM = GM_WIDTH // GM_GROUPS
N_HEADS = 8
HEAD_DIM = D_MODEL // N_HEADS
N_KV_HEADS = 2
HEADS_PER_KV = N_HEADS // N_KV_HEADS
KV_DIM = N_KV_HEADS * HEAD_DIM
ROT_DIM = HEAD_DIM // 4
IDX_HEADS = 8
IDX_DIM = 64
IDX_ROT_DIM = IDX_DIM // 4
TOPK_MAX = 256
N_EXPERT_GROUPS = 4
EXPERTS_PER_GROUP = 8
N_EXPERTS = N_EXPERT_GROUPS * EXPERTS_PER_GROUP
D_EXPERT = D_MODEL // 2
PLE_DIM = 256

LANES = 128
V7X_VMEM_LIMIT = 56 * 1024 * 1024

TM_MIX = 256
TM_TOK = 256
TM_RANK = 512
MOE_ROWS = 256
Q_BLOCK = 128
K_TILE = 512
ATT_GROUPS = 4
NEG_BIG = -1e30

R_E0, R_E1, R_G0, R_G1 = 0, 1, 2, 3
R_EXP0 = N_EXPERT_GROUPS


def _cparams(*sem):
    return pltpu.CompilerParams(dimension_semantics=sem, vmem_limit_bytes=V7X_VMEM_LIMIT)


def _const_spec(shape):
    nd = len(shape)
    return pl.BlockSpec(shape, lambda *_: (0,) * nd, pipeline_mode=pl.Buffered(1))


def _rms(x, g):
    return x * lax.rsqrt(jnp.mean(x * x, axis=-1, keepdims=True) + EPS) * g


def _gelu(x):
    return jax.nn.gelu(x)


def _dot(a, b):
    return jnp.dot(a, b, preferred_element_type=F32)


def _dot_nt(a, b):
    return lax.dot_general(a, b, (((1,), (1,)), ((), ())), preferred_element_type=F32)


def _split_bf16(a):
    hi = a.astype(BF16)
    lo = (a - hi.astype(F32)).astype(BF16)
    return hi, lo


def _dot_3pass(a, b):
    ah, al = _split_bf16(a)
    bh, bl = _split_bf16(b)
    return _dot(ah, bh) + (_dot(al, bh) + _dot(ah, bl))


def _route(xn, wr_ref, br_ref):
    tm = xn.shape[0]
    logits = _dot_3pass(xn, wr_ref[...]) + br_ref[...]
    lane = lax.broadcasted_iota(I32, (tm, LANES), 1)
    neg_inf = jnp.float32(-jnp.inf)

    gmask = lane < N_EXPERT_GROUPS
    gl = jnp.where(gmask, logits, neg_inf)
    gmax = jnp.max(gl, axis=-1, keepdims=True)
    g_sel = jnp.min(jnp.where(gl == gmax, lane, LANES), axis=-1, keepdims=True)
    p_group = 1.0 / jnp.sum(jnp.where(gmask, jnp.exp(gl - gmax), 0.0), axis=-1, keepdims=True)

    lo = R_EXP0 + EXPERTS_PER_GROUP * g_sel
    lmask = (lane >= lo) & (lane < lo + EXPERTS_PER_GROUP)
    ll = jnp.where(lmask, logits, neg_inf)
    lmax = jnp.max(ll, axis=-1, keepdims=True)
    le = jnp.where(lmask, jnp.exp(ll - lmax), 0.0)
    probs = jnp.where(lmask, le / jnp.sum(le, axis=-1, keepdims=True), -1.0)
    p1 = jnp.max(probs, axis=-1, keepdims=True)
    i1 = jnp.min(jnp.where(probs == p1, lane, LANES), axis=-1, keepdims=True)
    probs2 = jnp.where(lane == i1, -1.0, probs)
    p2 = jnp.max(probs2, axis=-1, keepdims=True)
    i2 = jnp.min(jnp.where(probs2 == p2, lane, LANES), axis=-1, keepdims=True)
    denom = p1 + p2
    g1 = p_group * p1 / denom
    g2 = p_group * p2 / denom
    e1 = (i1 - R_EXP0).astype(F32)
    e2 = (i2 - R_EXP0).astype(F32)
    out = jnp.where(lane == R_E0, e1, 0.0)
    out = jnp.where(lane == R_E1, e2, out)
    out = jnp.where(lane == R_G0, g1, out)
    out = jnp.where(lane == R_G1, g2, out)
    hits = jnp.where((lane == i1 - R_EXP0) | (lane == i2 - R_EXP0), 1.0, 0.0)
    counts = jnp.where(lax.broadcasted_iota(I32, (8, LANES), 0) == 0,
                       jnp.sum(hits, axis=0, keepdims=True), 0.0)
    return out, counts


def _router_weights(w_group, b_group, w_expert, b_expert):
    wr = jnp.zeros((D_MODEL, LANES), F32)
    wr = wr.at[:, :N_EXPERT_GROUPS].set(w_group).at[:, R_EXP0:R_EXP0 + N_EXPERTS].set(w_expert)
    br = jnp.zeros((1, LANES), F32)
    br = br.at[0, :N_EXPERT_GROUPS].set(b_group).at[0, R_EXP0:R_EXP0 + N_EXPERTS].set(b_expert)
    return wr, br


def _gmlp_kernel(x_ref, mixg_ref, win_ref, vgain_ref, ws_ref, bst_ref, wout_ref, ffng_ref, wr_ref,
                 br_ref, h_ref, xn_ref, route_ref, cnt_ref, xn_s, v_s, prod_s):
    tm = x_ref.shape[0]
    xn_s[...] = _rms(x_ref[...], mixg_ref[...]).astype(BF16)

    nt = 512
    ssq = jnp.zeros((tm, 1), F32)
    for j in range(GM_WIDTH // nt):
        z = _gelu(_dot(xn_s[...], win_ref[:, GM_WIDTH + j * nt:GM_WIDTH + (j + 1) * nt]))
        v_s[:, j * nt:(j + 1) * nt] = z
        ssq = ssq + jnp.sum(z * z, axis=-1, keepdims=True)
    rinv = lax.rsqrt(ssq * (1.0 / GM_WIDTH) + EPS)

    row = lax.broadcasted_iota(I32, (GM_CHUNK, GM_CHUNK), 0)
    col = lax.broadcasted_iota(I32, (GM_CHUNK, GM_CHUNK), 1)
    tril = row >= col
    gd = GM_GROUP_DIM
    for g in range(GM_GROUPS):
        wm = jnp.where(tril, ws_ref[g], 0.0).astype(BF16)
        vg = (v_s[:, g * gd:(g + 1) * gd] * rinv * vgain_ref[:, g * gd:(g + 1) * gd]).astype(BF16)
        u = _gelu(_dot(xn_s[...], win_ref[:, g * gd:(g + 1) * gd]))
        bias = bst_ref[:, g:g + 1]
        for c in range(tm // GM_CHUNK):
            rows = slice(c * GM_CHUNK, (c + 1) * GM_CHUNK)
            sv = _dot(wm, vg[rows, :]) + bias
            prod_s[rows, g * gd:(g + 1) * gd] = (u[rows, :] * sv).astype(BF16)

    h = x_ref[...] + _dot(prod_s[...], wout_ref[...])
    h_ref[...] = h
    xn = _rms(h, ffng_ref[...])
    xn_ref[...] = xn
    route_ref[...], cnt_ref[...] = _route(xn, wr_ref, br_ref)


def _gmlp_layer(x, mix_g, w_in, v_gain, w_s, b_s, w_out, ffn_g, wr, br):
    t = x.shape[0]
    tm = TM_MIX
    row_spec = pl.BlockSpec((tm, D_MODEL), lambda i: (i, 0))
    return pl.pallas_call(
        _gmlp_kernel,
        grid=(t // tm,),
        in_specs=[
            row_spec,
            _const_spec((1, D_MODEL)),
            _const_spec((D_MODEL, 2 * GM_WIDTH)),
            _const_spec((1, GM_WIDTH)),
            _const_spec((GM_GROUPS, GM_CHUNK, GM_CHUNK)),
            _const_spec((GM_CHUNK, GM_GROUPS)),
            _const_spec((GM_WIDTH, D_MODEL)),
            _const_spec((1, D_MODEL)),
            _const_spec((D_MODEL, LANES)),
            _const_spec((1, LANES)),
        ],
        out_specs=[row_spec, row_spec, pl.BlockSpec((tm, LANES), lambda i: (i, 0)),
                   pl.BlockSpec((8, LANES), lambda i: (i, 0))],
        out_shape=[jax.ShapeDtypeStruct((t, D_MODEL), F32), jax.ShapeDtypeStruct((t, D_MODEL), F32),
                   jax.ShapeDtypeStruct((t, LANES), F32), jax.ShapeDtypeStruct((t // tm * 8, LANES), F32)],
        scratch_shapes=[pltpu.VMEM((tm, D_MODEL), BF16), pltpu.VMEM((tm, GM_WIDTH), F32),
                        pltpu.VMEM((tm, GM_WIDTH), BF16)],
        compiler_params=_cparams("parallel"),
        name="gmlp_mixer",
    )(x, mix_g.reshape(1, -1), w_in.astype(BF16), v_gain.reshape(1, -1), w_s, b_s.T,
      w_out.astype(BF16), ffn_g.reshape(1, -1), wr, br)


def _rank_kernel(route_ref, start_ref, dest_ref, carry_s):
    tm = route_ref.shape[0]

    @pl.when(pl.program_id(0) == 0)
    def _():
        carry_s[...] = jnp.zeros_like(carry_s)

    r = route_ref[...]
    lane = lax.broadcasted_iota(I32, (tm, LANES), 1)
    lanef = lane.astype(F32)
    oh0 = (lanef == r[:, R_E0:R_E0 + 1]).astype(F32)
    oh1 = (lanef == r[:, R_E1:R_E1 + 1]).astype(F32)
    both = oh0 + oh1
    row = lax.broadcasted_iota(I32, (tm, tm), 0)
    col = lax.broadcasted_iota(I32, (tm, tm), 1)
    strict_lower = (row > col).astype(BF16)
    base = start_ref[...] + carry_s[0:1, :] + _dot(strict_lower, both.astype(BF16))
    d0 = jnp.sum(oh0 * base, axis=-1, keepdims=True)
    d1 = jnp.sum(oh1 * (base + oh0), axis=-1, keepdims=True)
    cols = jnp.where(lane == 0, d0, jnp.where(lane == 1, d1, 0.0))
    dest_ref[...] = cols.T[0:8, :].astype(I32)
    carry_s[0:1, :] = carry_s[0:1, :] + jnp.sum(both, axis=0, keepdims=True)


def _moe_rank(route, start):
    t = route.shape[0]
    tm = min(TM_RANK, t)
    nt = t // tm
    dest = pl.pallas_call(
        _rank_kernel,
        grid=(nt,),
        in_specs=[pl.BlockSpec((tm, LANES), lambda i: (i, 0)), _const_spec((1, LANES))],
        out_specs=pl.BlockSpec((8, tm), lambda i: (i, 0)),
        out_shape=jax.ShapeDtypeStruct((nt * 8, tm), I32),
        scratch_shapes=[pltpu.VMEM((8, LANES), F32)],
        compiler_params=_cparams("arbitrary"),
        name="moe_rank",
    )(route, start)
    return dest.reshape(nt, 8, tm)[:, :2, :]


def _row_copy(src_ref, dst_ref, sem, s, d):
    return pltpu.make_async_copy(src_ref.at[pl.ds(s, 1)], dst_ref.at[pl.ds(d, 1)], sem)


def _dispatch_kernel(dest_ref, xn_ref, bufx_ref, sem):
    tm = TM_TOK
    i = pl.program_id(0)

    def drain():
        for _ in range(2 * tm):
            _row_copy(xn_ref, bufx_ref, sem, 0, 0).wait()

    @pl.when(i > 0)
    def _():
        drain()

    for t in range(tm):
        for k in range(2):
            _row_copy(xn_ref, bufx_ref, sem, i * tm + t, dest_ref[k * tm + t]).start()

    @pl.when(i == pl.num_programs(0) - 1)
    def _():
        drain()


def _moe_dispatch(xn, dest_flat):
    t = xn.shape[0]
    tm = TM_TOK
    return pl.pallas_call(
        _dispatch_kernel,
        grid=(t // tm,),
        in_specs=[pl.BlockSpec((2 * tm,), lambda i: (i,), memory_space=pltpu.SMEM),
                  pl.BlockSpec(memory_space=pl.ANY)],
        out_specs=pl.BlockSpec(memory_space=pl.ANY),
        out_shape=jax.ShapeDtypeStruct((2 * t, D_MODEL), F32),
        scratch_shapes=[pltpu.SemaphoreType.DMA(())],
        compiler_params=_cparams("arbitrary"),
        name="moe_dispatch",
    )(dest_flat, xn)


def _expert_kernel(ie_ref, ib_ref, ilo_ref, ihi_ref, ifirst_ref, n_ref,
                   x_ref, w1_ref, w3_ref, w2_ref, y_ref, w1_s, w3_s, w2_s):
    w = pl.program_id(0)
    live = w < n_ref[0]
    e = ie_ref[w]
    prev = ie_ref[jnp.maximum(w - 1, 0)]

    @pl.when(live & ((w == 0) | (e != prev)))
    def _():
        w1_s[...] = w1_ref[0].astype(BF16)
        w3_s[...] = w3_ref[0].astype(BF16)
        w2_s[...] = w2_ref[0].astype(BF16)

    def ffn():
        rows = lax.broadcasted_iota(I32, (x_ref.shape[0], 1), 0)
        mine = (rows >= ilo_ref[w]) & (rows < ihi_ref[w])
        x = jnp.where(mine, x_ref[...], 0.0).astype(BF16)
        a = _dot(x, w1_s[...])
        b = _dot(x, w3_s[...])
        mid = (a * jax.nn.sigmoid(a) * b).astype(BF16)
        return _dot(mid, w2_s[...])

    @pl.when(live & (ifirst_ref[w] == 1))
    def _():
        y_ref[...] = ffn()

    @pl.when(live & (ifirst_ref[w] == 0))
    def _():
        y_ref[...] += ffn()


def _moe_experts(bufx, items, w1, w3, w2, layer):
    n_rows = bufx.shape[0]
    br = MOE_ROWS
    n_items_max = items[0].shape[0]

    def row_map(w, ie, ib, ilo, ihi, ifirst, n):
        return (ib[w], 0)

    def w_map(w, ie, ib, ilo, ihi, ifirst, n):
        return (layer, ie[w], 0, 0)

    grid_spec = pltpu.PrefetchScalarGridSpec(
        num_scalar_prefetch=6,
        grid=(n_items_max,),
        in_specs=[pl.BlockSpec((br, D_MODEL), row_map),
                  pl.BlockSpec((None, 1, D_MODEL, D_EXPERT), w_map),
                  pl.BlockSpec((None, 1, D_MODEL, D_EXPERT), w_map),
                  pl.BlockSpec((None, 1, D_EXPERT, D_MODEL), w_map)],
        out_specs=pl.BlockSpec((br, D_MODEL), row_map),
        scratch_shapes=[pltpu.VMEM((D_MODEL, D_EXPERT), BF16), pltpu.VMEM((D_MODEL, D_EXPERT), BF16),
                        pltpu.VMEM((D_EXPERT, D_MODEL), BF16)],
    )
    return pl.pallas_call(
        _expert_kernel,
        grid_spec=grid_spec,
        out_shape=jax.ShapeDtypeStruct((n_rows, D_MODEL), F32),
        compiler_params=_cparams("arbitrary"),
        name="moe_experts",
    )(*items, bufx, w1, w3, w2)


def _moe_plan(counts, t):
    br = MOE_ROWS
    nb = 2 * t // br
    n_items_max = nb + N_EXPERTS - 1
    cnt = counts[:N_EXPERTS].astype(I32)
    end = jnp.cumsum(cnt)
    start = end - cnt

    first_blk = start // br
    n_blk = jnp.where(cnt > 0, (end - 1) // br - first_blk + 1, 0)
    iend = jnp.cumsum(n_blk)
    istart = iend - n_blk
    n_items = iend[-1:]
    w = jnp.minimum(jnp.arange(n_items_max, dtype=I32), n_items[0] - 1)
    ie = jnp.minimum(jnp.sum((iend[None, :] <= w[:, None]).astype(I32), axis=1), N_EXPERTS - 1)
    ib = first_blk[ie] + (w - istart[ie])
    ilo = jnp.maximum(start[ie], ib * br) - ib * br
    ihi = jnp.minimum(end[ie], (ib + 1) * br) - ib * br
    ifirst = jnp.concatenate([jnp.ones((1,), I32), (ib[1:] != ib[:-1]).astype(I32)])
    items = tuple(a.astype(I32) for a in (ie, ib, ilo, ihi, ifirst, n_items))
    return items


def _moe(xn, route, tile_counts, w1, w3, w2, layer):
    t = xn.shape[0]
    counts = jnp.sum(tile_counts.reshape(-1, 8, LANES)[:, 0, :], axis=0)
    start = (jnp.cumsum(counts) - counts).reshape(1, LANES)
    dest = _moe_rank(route, start)
    nt_r, _, tm_r = dest.shape
    dest = dest.reshape(nt_r, 2, tm_r // TM_TOK, TM_TOK).transpose(0, 2, 1, 3).reshape(-1)
    items = _moe_plan(counts, t)
    bufx = _moe_dispatch(xn, dest)
    bufy = _moe_experts(bufx, items, w1, w3, w2, layer)
    return bufy, dest


def _ple_kernel(dest_ref, dest_next_ref, h_ref, route_ref, p_ref, bufy_ref, pleg_ref, wg_ref, wp_ref,
                fing_ref, out_ref, y_s, sem, *, final):
    tm = h_ref.shape[0]
    i = pl.program_id(0)
    last = pl.num_programs(0) - 1
    slot = lax.rem(i, 2)

    def gather(dref, s):
        for t in range(tm):
            for k in range(2):
                _row_copy(bufy_ref, y_s.at[s, k], sem.at[s], dref[k * tm + t], t).start(priority=k)

    def drain(s):
        for _ in range(2 * tm):
            _row_copy(bufy_ref, y_s.at[s, 0], sem.at[s], 0, 0).wait()

    @pl.when(i == 0)
    def _():
        gather(dest_ref, 0)

    drain(slot)
    gather(dest_next_ref, 1 - slot)

    r = route_ref[...]
    h = h_ref[...] + (y_s[slot, 0] * r[:, R_G0:R_G0 + 1] + y_s[slot, 1] * r[:, R_G1:R_G1 + 1])
    gate = jax.nn.sigmoid(_dot(_rms(h, pleg_ref[...]).astype(BF16), wg_ref[...]))
    h = h + gate * _dot(p_ref[...].astype(BF16), wp_ref[...])
    out_ref[...] = _rms(h, fing_ref[...]) if final else h

    @pl.when(i == last)
    def _():
        drain(1 - slot)


def _moe_combine_ple(h, route, bufy, dest, p_all, layer, ple_g, w_gate, w_proj, final_g, final):
    t = h.shape[0]
    tm = TM_TOK
    nt = t // tm
    row_spec = pl.BlockSpec((tm, D_MODEL), lambda i: (i, 0))
    return pl.pallas_call(
        functools.partial(_ple_kernel, final=final),
        grid=(nt,),
        in_specs=[pl.BlockSpec((2 * tm,), lambda i: (i,), memory_space=pltpu.SMEM),
                  pl.BlockSpec((2 * tm,), lambda i: (jnp.minimum(i + 1, nt - 1),), memory_space=pltpu.SMEM),
                  row_spec,
                  pl.BlockSpec((tm, LANES), lambda i: (i, 0)),
                  pl.BlockSpec((None, tm, PLE_DIM), lambda i: (layer, i, 0)),
                  pl.BlockSpec(memory_space=pl.ANY),
                  _const_spec((1, D_MODEL)),
                  _const_spec((D_MODEL, D_MODEL)),
                  _const_spec((PLE_DIM, D_MODEL)),
                  _const_spec((1, D_MODEL))],
        out_specs=row_spec,
        out_shape=jax.ShapeDtypeStruct((t, D_MODEL), F32),
        scratch_shapes=[pltpu.VMEM((2, 2, tm, D_MODEL), F32), pltpu.SemaphoreType.DMA((2,))],
        compiler_params=_cparams("arbitrary"),
        name="moe_combine_ple",
    )(dest, dest, h, route, p_all, bufy, ple_g.reshape(1, -1), w_gate.astype(BF16), w_proj.astype(BF16),
      final_g.reshape(1, -1))


_P_Q = 0
_P_K = _P_Q + N_HEADS * HEAD_DIM
_P_V = _P_K + KV_DIM
_P_QI = _P_V + KV_DIM
_P_KI = _P_QI + IDX_HEADS * LANES
_P_WI = _P_KI + LANES
_P_END = _P_WI + LANES


def _rope_tables(pos, inv_ref, half):
    tm = pos.shape[0]
    lane = lax.broadcasted_iota(I32, (tm, LANES), 1)
    ang = pos * inv_ref[...]
    rot = lane < 2 * half
    cos = jnp.where(rot, jnp.cos(ang), 1.0)
    sin = jnp.sin(ang)
    s_lo = jnp.where(lane < half, -sin, 0.0)
    s_hi = jnp.where(rot & (lane >= half), sin, 0.0)
    return cos, s_lo, s_hi


def _rope(x, tables, half):
    cos, s_lo, s_hi = tables
    return x * cos + pltpu.roll(x, LANES - half, 1) * s_lo + pltpu.roll(x, half, 1) * s_hi


def _dsa_in_kernel(h_ref, g_ref, pos_ref, w_ref, invq_ref, invi_ref,
                   q_ref, k_ref, v_ref, qi_ref, ki_ref, wi_ref, hn_s):
    hn_s[...] = _rms(h_ref[...], g_ref[...]).astype(BF16)
    pos = pos_ref[...].astype(F32)
    tq = _rope_tables(pos, invq_ref, ROT_DIM // 2)
    ti = _rope_tables(pos, invi_ref, IDX_ROT_DIM // 2)

    def proj(c0, width):
        return _dot(hn_s[...], w_ref[:, c0:c0 + width])

    def rope_blocks(z, out_ref, c_out, tables, half):
        for b in range(z.shape[1] // LANES):
            blk = _rope(z[:, b * LANES:(b + 1) * LANES], tables, half)
            out_ref[:, c_out + b * LANES:c_out + (b + 1) * LANES] = blk.astype(BF16)

    nw = 512
    for c in range(0, N_HEADS * HEAD_DIM, nw):
        rope_blocks(proj(_P_Q + c, nw), q_ref, c, tq, ROT_DIM // 2)
    kv = proj(_P_K, 2 * KV_DIM)
    rope_blocks(kv[:, :KV_DIM], k_ref, 0, tq, ROT_DIM // 2)
    v_ref[...] = kv[:, KV_DIM:].astype(BF16)
    for c in range(0, IDX_HEADS * LANES, nw):
        rope_blocks(proj(_P_QI + c, nw), qi_ref, c, ti, IDX_ROT_DIM // 2)
    kw = proj(_P_KI, 2 * LANES)
    rope_blocks(kw[:, :LANES], ki_ref, 0, ti, IDX_ROT_DIM // 2)
    wi_ref[...] = kw[:, LANES:] * (IDX_HEADS ** -0.5 * IDX_DIM ** -0.5)


def _inv_freq_lanes(rot_dim):
    half = rot_dim // 2
    inv = ROPE_THETA ** (-jnp.arange(0, rot_dim, 2, dtype=F32) / rot_dim)
    lanes = jnp.zeros((LANES,), F32).at[:rot_dim].set(jnp.concatenate([inv, inv]))
    del half
    return lanes.reshape(1, LANES)


def _dsa_in_weights(w_in):
    d = w_in.shape[0]
    c_q, c_k, c_v = N_HEADS * HEAD_DIM, KV_DIM, KV_DIM
    c_qi, c_ki = IDX_HEADS * IDX_DIM, IDX_DIM
    o = np.cumsum([0, c_q, c_k, c_v, c_qi, c_ki, IDX_HEADS])
    wq, wk, wv, wqi, wki, wwi = (w_in[:, o[i]:o[i + 1]] for i in range(6))
    wqi = jnp.pad(wqi.reshape(d, IDX_HEADS, IDX_DIM), ((0, 0), (0, 0), (0, LANES - IDX_DIM)))
    wki = jnp.pad(wki, ((0, 0), (0, LANES - IDX_DIM)))
    wwi = jnp.pad(wwi, ((0, 0), (0, LANES - IDX_HEADS)))
    return jnp.concatenate([wq, wk, wv, wqi.reshape(d, IDX_HEADS * LANES), wki, wwi], axis=1).astype(BF16)


def _dsa_in(h, mix_g, positions, w_in):
    t = h.shape[0]
    tm = min(TM_MIX, t)
    spec = lambda w: pl.BlockSpec((tm, w), lambda i: (i, 0))
    widths = (N_HEADS * HEAD_DIM, KV_DIM, KV_DIM, IDX_HEADS * LANES, LANES)
    return pl.pallas_call(
        _dsa_in_kernel,
        grid=(t // tm,),
        in_specs=[spec(D_MODEL), _const_spec((1, D_MODEL)), spec(1), _const_spec((D_MODEL, _P_END)),
                  _const_spec((1, LANES)), _const_spec((1, LANES))],
        out_specs=[spec(w) for w in widths] + [spec(LANES)],
        out_shape=[jax.ShapeDtypeStruct((t, w), BF16) for w in widths]
        + [jax.ShapeDtypeStruct((t, LANES), F32)],
        scratch_shapes=[pltpu.VMEM((tm, D_MODEL), BF16)],
        compiler_params=_cparams("parallel"),
        name="dsa_in_proj",
    )(h, mix_g.reshape(1, -1), positions.reshape(t, 1), _dsa_in_weights(w_in),
      _inv_freq_lanes(ROT_DIM), _inv_freq_lanes(IDX_ROT_DIM))


_INT_MIN = -2 ** 31
_U_NEG_INF = 0x007FFFFF


def _key_to_f32(u):
    bits = jnp.where(u < 0, u ^ _INT_MIN, ~u)
    return lax.bitcast_convert_type(bits, F32)


def _dsa_core_kernel(q_ref, qi_ref, wi_ref, k_ref, v_ref, ki_ref, o_ref,
                     score_s, bias_s, mx_s, l_s, acc_s, *, q0, k_top):
    qb = q_ref.shape[0]
    sk = k_ref.shape[0]
    n_kt = sk // K_TILE
    n_ch = sk // LANES
    qpos = (q0 + pl.program_id(1)) * qb + lax.broadcasted_iota(I32, (qb, 1), 0)
    neg_inf = jnp.float32(-jnp.inf)

    wcols = [wi_ref[:, hd:hd + 1] for hd in range(IDX_HEADS)]

    def score_tile(kt, c):
        k0 = pl.multiple_of(kt * K_TILE, K_TILE)
        kit = ki_ref[pl.ds(k0, K_TILE), :]
        acc = jnp.zeros((qb, K_TILE), F32)
        for hd in range(IDX_HEADS):
            rel = jnp.maximum(_dot_nt(qi_ref[:, hd * LANES:(hd + 1) * LANES], kit), 0.0)
            acc = acc + wcols[hd] * rel
        kpos = k0 + lax.broadcasted_iota(I32, (qb, K_TILE), 1)
        score_s[:, pl.ds(k0, K_TILE)] = jnp.where(kpos <= qpos, acc, neg_inf)
        return c

    lax.fori_loop(0, n_kt, score_tile, 0)

    def count_ge(thr):
        acc = jnp.zeros((qb, LANES), F32)
        for c in range(n_ch):
            acc = acc + jnp.where(score_s[:, c * LANES:(c + 1) * LANES] >= thr, 1.0, 0.0)
        return jnp.sum(acc, axis=-1, keepdims=True)

    kf = jnp.float32(k_top)

    def bisect(p, ans):
        bit = lax.shift_left(jnp.int32(1), 31 - p)
        cand = ans | bit
        force = (cand >= 0) & (cand <= _U_NEG_INF)
        ok = (count_ge(_key_to_f32(cand)) >= kf) | force
        return jnp.where(ok, cand, ans)

    ans = lax.fori_loop(0, 32, bisect, jnp.zeros((qb, 1), I32))
    thr = _key_to_f32(jnp.where(ans == _U_NEG_INF, ans + 1, ans))
    thr_up = _key_to_f32(ans + 1)

    c_hi = count_ge(thr_up)
    need0 = kf - c_hi
    rem0 = count_ge(thr) - c_hi
    for c in range(n_ch):
        s = score_s[:, c * LANES:(c + 1) * LANES]
        bias_s[:, c * LANES:(c + 1) * LANES] = jnp.where(s >= thr_up, 0.0, NEG_BIG)

    def active_of(need, rem):
        return (need > 0.0) & (rem > 0.0)

    def bin_cond(carry):
        return carry[2] > 0.0

    def bin_body(carry):
        need, rem, _ = carry
        active = active_of(need, rem)
        best = jnp.full((qb, LANES), neg_inf, F32)
        for c in range(n_ch):
            s = score_s[:, c * LANES:(c + 1) * LANES]
            b = bias_s[:, c * LANES:(c + 1) * LANES]
            best = jnp.maximum(best, jnp.where((s >= thr) & (b != 0.0), s, neg_inf))
        m = jnp.max(best, axis=-1, keepdims=True)
        first = jnp.full((qb, LANES), sk, I32)
        lane = lax.broadcasted_iota(I32, (qb, LANES), 1)
        for c in range(n_ch):
            s = score_s[:, c * LANES:(c + 1) * LANES]
            b = bias_s[:, c * LANES:(c + 1) * LANES]
            hit = (s == m) & (s >= thr) & (b != 0.0)
            first = jnp.minimum(first, jnp.where(hit, lane + c * LANES, sk))
        pick = jnp.min(first, axis=-1, keepdims=True)
        for c in range(n_ch):
            b = bias_s[:, c * LANES:(c + 1) * LANES]
            sel = active & (lane + c * LANES == pick)
            bias_s[:, c * LANES:(c + 1) * LANES] = jnp.where(sel, 0.0, b)
        step = jnp.where(active, 1.0, 0.0)
        need, rem = need - step, rem - step
        go = jnp.max(jnp.where(active_of(need, rem), 1.0, 0.0))
        return need, rem, go

    go0 = jnp.max(jnp.where(active_of(need0, rem0), 1.0, 0.0))
    lax.while_loop(bin_cond, bin_body, (need0, rem0, go0))

    scale = HEAD_DIM ** -0.5
    n_fold = K_TILE // LANES

    def q_rows(g):
        return jnp.concatenate(
            [q_ref[:, (g * HEADS_PER_KV + r) * HEAD_DIM:(g * HEADS_PER_KV + r + 1) * HEAD_DIM]
             for r in range(HEADS_PER_KV)], axis=0)

    def masked_logits(g, k0):
        kt_ = k_ref[pl.ds(k0, K_TILE), g * HEAD_DIM:(g + 1) * HEAD_DIM]
        bias = bias_s[:, pl.ds(k0, K_TILE)]
        return _dot_nt(q_rows(g), kt_) * scale + jnp.concatenate([bias] * HEADS_PER_KV, axis=0)

    def fold(x, op):
        out = x[:, 0:LANES]
        for c in range(1, n_fold):
            out = op(out, x[:, c * LANES:(c + 1) * LANES])
        return out

    mx_s[...] = jnp.full(mx_s.shape, NEG_BIG, F32)
    l_s[...] = jnp.zeros(l_s.shape, F32)
    acc_s[...] = jnp.zeros(acc_s.shape, F32)

    def max_tile(kt, c):
        k0 = pl.multiple_of(kt * K_TILE, K_TILE)
        for g in range(N_KV_HEADS):
            mx_s[g] = jnp.maximum(mx_s[g], fold(masked_logits(g, k0), jnp.maximum))
        return c

    lax.fori_loop(0, n_kt, max_tile, 0)
    row_max = [jnp.max(mx_s[g], axis=-1, keepdims=True) for g in range(N_KV_HEADS)]

    def att_tile(kt, c):
        k0 = pl.multiple_of(kt * K_TILE, K_TILE)
        for g in range(N_KV_HEADS):
            vt_ = v_ref[pl.ds(k0, K_TILE), g * HEAD_DIM:(g + 1) * HEAD_DIM]
            pexp = jnp.exp(masked_logits(g, k0) - row_max[g])
            l_s[g] = l_s[g] + fold(pexp, jnp.add)
            acc_s[g] = acc_s[g] + _dot(pexp.astype(BF16), vt_)
        return c

    lax.fori_loop(0, n_kt, att_tile, 0)
    for g in range(N_KV_HEADS):
        og = acc_s[g] / jnp.sum(l_s[g], axis=-1, keepdims=True)
        for r in range(HEADS_PER_KV):
            c0 = (g * HEADS_PER_KV + r) * HEAD_DIM
            o_ref[:, c0:c0 + HEAD_DIM] = og[r * qb:(r + 1) * qb, :].astype(BF16)


def _dsa_core(q, k, v, qi, ki, wi, batch, seq):
    qb = Q_BLOCK
    k_top = min(TOPK_MAX, seq // 4)
    nq = seq // qb
    groups = ATT_GROUPS if nq % ATT_GROUPS == 0 and (seq // ATT_GROUPS) % K_TILE == 0 else 1
    gq = nq // groups
    view = lambda a: a.reshape(batch, seq, a.shape[-1])
    q3, k3, v3, qi3, ki3, wi3 = (view(a) for a in (q, k, v, qi, ki, wi))
    outs = []
    for gi in range(groups):
        sk = (gi + 1) * gq * qb
        q0 = gi * gq
        qspec = lambda w, q0=q0: pl.BlockSpec((None, qb, w), lambda b, j: (b, q0 + j, 0))
        kspec = lambda w, sk=sk: pl.BlockSpec((None, sk, w), lambda b, j: (b, 0, 0))
        rows = HEADS_PER_KV * qb
        outs.append(pl.pallas_call(
            functools.partial(_dsa_core_kernel, q0=q0, k_top=k_top),
            grid=(batch, gq),
            in_specs=[qspec(D_MODEL), qspec(IDX_HEADS * LANES), qspec(LANES),
                      kspec(KV_DIM), kspec(KV_DIM), kspec(LANES)],
            out_specs=pl.BlockSpec((None, qb, D_MODEL), lambda b, j: (b, j, 0)),
            out_shape=jax.ShapeDtypeStruct((batch, gq * qb, D_MODEL), BF16),
            scratch_shapes=[pltpu.VMEM((qb, sk), F32), pltpu.VMEM((qb, sk), F32),
                            pltpu.VMEM((N_KV_HEADS, rows, LANES), F32),
                            pltpu.VMEM((N_KV_HEADS, rows, LANES), F32),
                            pltpu.VMEM((N_KV_HEADS, rows, HEAD_DIM), F32)],
            compiler_params=_cparams("parallel", "arbitrary"),
            name=f"dsa_core_{gi}",
        )(q3, qi3, wi3, k3, v3, ki3))
    out = outs[0] if groups == 1 else jnp.concatenate(outs, axis=1)
    return out.reshape(batch * seq, D_MODEL)


def _dsa_out_kernel(h_ref, a_ref, wo_ref, ffng_ref, wr_ref, br_ref, h_out_ref, xn_ref, route_ref,
                    cnt_ref):
    h = h_ref[...] + _dot(a_ref[...], wo_ref[...])
    h_out_ref[...] = h
    xn = _rms(h, ffng_ref[...])
    xn_ref[...] = xn
    route_ref[...], cnt_ref[...] = _route(xn, wr_ref, br_ref)


def _dsa_out(h, att, w_out, ffn_g, wr, br):
    t = h.shape[0]
    tm = min(TM_MIX, t)
    row_spec = pl.BlockSpec((tm, D_MODEL), lambda i: (i, 0))
    return pl.pallas_call(
        _dsa_out_kernel,
        grid=(t // tm,),
        in_specs=[row_spec, row_spec, _const_spec((D_MODEL, D_MODEL)), _const_spec((1, D_MODEL)),
                  _const_spec((D_MODEL, LANES)), _const_spec((1, LANES))],
        out_specs=[row_spec, row_spec, pl.BlockSpec((tm, LANES), lambda i: (i, 0)),
                   pl.BlockSpec((8, LANES), lambda i: (i, 0))],
        out_shape=[jax.ShapeDtypeStruct((t, D_MODEL), F32), jax.ShapeDtypeStruct((t, D_MODEL), F32),
                   jax.ShapeDtypeStruct((t, LANES), F32), jax.ShapeDtypeStruct((t // tm * 8, LANES), F32)],
        compiler_params=_cparams("parallel"),
        name="dsa_out_proj",
    )(h, att, w_out.astype(BF16), ffn_g.reshape(1, -1), wr, br)


def kernel(x, p, positions, mix_norm, a_w_in, a_v_gain, a_w_s, a_b_s, a_w_out, b_w_in, b_w_out,
           ffn_norm, moe_w_group, moe_b_group, moe_w_expert, moe_b_expert, moe_w1, moe_w3, moe_w2,
           ple_norm, ple_w_gate, ple_w_proj, final_norm):
    batch, seq, d = x.shape
    depth = mix_norm.shape[0]
    t = batch * seq
    h = x.reshape(t, d)
    for i in range(depth):
        j = i // 2
        wr, br = _router_weights(moe_w_group[i], moe_b_group[i], moe_w_expert[i], moe_b_expert[i])
        if i % 2 == 0:
            h, xn, route, tile_counts = _gmlp_layer(h, mix_norm[i], a_w_in[j], a_v_gain[j], a_w_s[j],
                                                    a_b_s[j], a_w_out[j], ffn_norm[i], wr, br)
        else:
            q, k, v, qi, ki, wi = _dsa_in(h, mix_norm[i], positions, b_w_in[j])
            att = _dsa_core(q, k, v, qi, ki, wi, batch, seq)
            h, xn, route, tile_counts = _dsa_out(h, att, b_w_out[j], ffn_norm[i], wr, br)
        bufy, dest = _moe(xn, route, tile_counts, moe_w1, moe_w3, moe_w2, i)
        h = _moe_combine_ple(h, route, bufy, dest, p.reshape(depth, t, -1), i, ple_norm[i],
                             ple_w_gate[i], ple_w_proj[i], final_norm, final=(i == depth - 1))
    return h.reshape(batch, seq, d)
```

```python
import functools

import numpy as np
import jax
import jax.numpy as jnp
from jax import lax
from jax.experimental import pallas as pl
from jax.experimental.pallas import tpu as pltpu

F32 = jnp.float32
BF16 = jnp.bfloat16
I32 = jnp.int32

D_MODEL = 1024
EPS = 1e-6
ROPE_THETA = 500000.0
GM_CHUNK = 128
GM_WIDTH = 2 * D_MODEL
GM_GROUPS = 8
GM_GROUP_DIM = GM_WIDTH // GM_GROUPS
N_HEADS = 8
HEAD_DIM = D_MODEL // N_HEADS
N_KV_HEADS = 2
HEADS_PER_KV = N_HEADS // N_KV_HEADS
KV_DIM = N_KV_HEADS * HEAD_DIM
ROT_DIM = HEAD_DIM // 4
IDX_HEADS = 8
IDX_DIM = 64
IDX_ROT_DIM = IDX_DIM // 4
TOPK_MAX = 256
N_EXPERT_GROUPS = 4
EXPERTS_PER_GROUP = 8
N_EXPERTS = N_EXPERT_GROUPS * EXPERTS_PER_GROUP
D_EXPERT = D_MODEL // 2
PLE_DIM = 256

LANES = 128
V7X_VMEM_LIMIT = 56 * 1024 * 1024

TM_MIX = 256
TM_TOK = 256
TM_RANK = 512
MOE_ROWS = 256
Q_BLOCK = 128
K_TILE = 512
NEG_BIG = -1e30

R_E0, R_E1, R_G0, R_G1 = 0, 1, 2, 3
R_EXP0 = N_EXPERT_GROUPS


def _cparams(*sem):
    return pltpu.CompilerParams(dimension_semantics=sem, vmem_limit_bytes=V7X_VMEM_LIMIT)


def _const_spec(shape):
    nd = len(shape)
    return pl.BlockSpec(shape, lambda *_: (0,) * nd, pipeline_mode=pl.Buffered(1))


def _rms(x, g):
    return x * lax.rsqrt(jnp.mean(x * x, axis=-1, keepdims=True) + EPS) * g


def _gelu(x):
    return jax.nn.gelu(x)


def _dot(a, b):
    return jnp.dot(a, b, preferred_element_type=F32)


def _dot_nt(a, b):
    return lax.dot_general(a, b, (((1,), (1,)), ((), ())), preferred_element_type=F32)


def _split_bf16(a):
    hi = a.astype(BF16)
    lo = (a - hi.astype(F32)).astype(BF16)
    return hi, lo


def _dot_3pass(a, b):
    ah, al = _split_bf16(a)
    bh, bl = _split_bf16(b)
    return _dot(ah, bh) + (_dot(al, bh) + _dot(ah, bl))


def _route(xn, wr_ref, br_ref):
    tm = xn.shape[0]
    logits = _dot_3pass(xn, wr_ref[...]) + br_ref[...]
    lane = lax.broadcasted_iota(I32, (tm, LANES), 1)
    neg_inf = jnp.float32(-jnp.inf)

    gmask = lane < N_EXPERT_GROUPS
    gl = jnp.where(gmask, logits, neg_inf)
    gmax = jnp.max(gl, axis=-1, keepdims=True)
    g_sel = jnp.min(jnp.where(gl == gmax, lane, LANES), axis=-1, keepdims=True)
    p_group = 1.0 / jnp.sum(jnp.where(gmask, jnp.exp(gl - gmax), 0.0), axis=-1, keepdims=True)

    lo = R_EXP0 + EXPERTS_PER_GROUP * g_sel
    lmask = (lane >= lo) & (lane < lo + EXPERTS_PER_GROUP)
    ll = jnp.where(lmask, logits, neg_inf)
    lmax = jnp.max(ll, axis=-1, keepdims=True)
    le = jnp.where(lmask, jnp.exp(ll - lmax), 0.0)
    probs = jnp.where(lmask, le / jnp.sum(le, axis=-1, keepdims=True), -1.0)
    p1 = jnp.max(probs, axis=-1, keepdims=True)
    i1 = jnp.min(jnp.where(probs == p1, lane, LANES), axis=-1, keepdims=True)
    probs2 = jnp.where(lane == i1, -1.0, probs)
    p2 = jnp.max(probs2, axis=-1, keepdims=True)
    i2 = jnp.min(jnp.where(probs2 == p2, lane, LANES), axis=-1, keepdims=True)
    denom = p1 + p2
    g1 = p_group * p1 / denom
    g2 = p_group * p2 / denom
    e1 = (i1 - R_EXP0).astype(F32)
    e2 = (i2 - R_EXP0).astype(F32)
    out = jnp.where(lane == R_E0, e1, 0.0)
    out = jnp.where(lane == R_E1, e2, out)
    out = jnp.where(lane == R_G0, g1, out)
    out = jnp.where(lane == R_G1, g2, out)
    hits = jnp.where((lane == i1 - R_EXP0) | (lane == i2 - R_EXP0), 1.0, 0.0)
    counts = jnp.where(lax.broadcasted_iota(I32, (8, LANES), 0) == 0,
                       jnp.sum(hits, axis=0, keepdims=True), 0.0)
    return out, counts


def _router_weights(w_group, b_group, w_expert, b_expert):
    wr = jnp.zeros((D_MODEL, LANES), F32)
    wr = wr.at[:, :N_EXPERT_GROUPS].set(w_group).at[:, R_EXP0:R_EXP0 + N_EXPERTS].set(w_expert)
    br = jnp.zeros((1, LANES), F32)
    br = br.at[0, :N_EXPERT_GROUPS].set(b_group).at[0, R_EXP0:R_EXP0 + N_EXPERTS].set(b_expert)
    return wr, br


def _gmlp_kernel(x_ref, mixg_ref, win_ref, vgain_ref, ws_ref, bst_ref, wout_ref, ffng_ref, wr_ref,
                 br_ref, h_ref, xn_ref, route_ref, cnt_ref, xn_s, v_s, prod_s):
    tm = x_ref.shape[0]
    xn_s[...] = _rms(x_ref[...], mixg_ref[...]).astype(BF16)

    nt = 512
    ssq = jnp.zeros((tm, 1), F32)
    for j in range(GM_WIDTH // nt):
        z = _gelu(_dot(xn_s[...], win_ref[:, GM_WIDTH + j * nt:GM_WIDTH + (j + 1) * nt]))
        v_s[:, j * nt:(j + 1) * nt] = z
        ssq = ssq + jnp.sum(z * z, axis=-1, keepdims=True)
    rinv = lax.rsqrt(ssq * (1.0 / GM_WIDTH) + EPS)

    row = lax.broadcasted_iota(I32, (GM_CHUNK, GM_CHUNK), 0)
    col = lax.broadcasted_iota(I32, (GM_CHUNK, GM_CHUNK), 1)
    tril = row >= col
    gd = GM_GROUP_DIM
    for g in range(GM_GROUPS):
        wm = jnp.where(tril, ws_ref[g], 0.0).astype(BF16)
        vg = (v_s[:, g * gd:(g + 1) * gd] * rinv * vgain_ref[:, g * gd:(g + 1) * gd]).astype(BF16)
        u = _gelu(_dot(xn_s[...], win_ref[:, g * gd:(g + 1) * gd]))
        bias = bst_ref[:, g:g + 1]
        for c in range(tm // GM_CHUNK):
            rows = slice(c * GM_CHUNK, (c + 1) * GM_CHUNK)
            sv = _dot(wm, vg[rows, :]) + bias
            prod_s[rows, g * gd:(g + 1) * gd] = (u[rows, :] * sv).astype(BF16)

    h = x_ref[...] + _dot(prod_s[...], wout_ref[...])
    h_ref[...] = h
    xn = _rms(h, ffng_ref[...])
    xn_ref[...] = xn
    route_ref[...], cnt_ref[...] = _route(xn, wr_ref, br_ref)


def _gmlp_layer(x, mix_g, w_in, v_gain, w_s, b_s, w_out, ffn_g, wr, br):
    t = x.shape[0]
    tm = TM_MIX
    row_spec = pl.BlockSpec((tm, D_MODEL), lambda i: (i, 0))
    return pl.pallas_call(
        _gmlp_kernel,
        grid=(t // tm,),
        in_specs=[
            row_spec,
            _const_spec((1, D_MODEL)),
            _const_spec((D_MODEL, 2 * GM_WIDTH)),
            _const_spec((1, GM_WIDTH)),
            _const_spec((GM_GROUPS, GM_CHUNK, GM_CHUNK)),
            _const_spec((GM_CHUNK, GM_GROUPS)),
            _const_spec((GM_WIDTH, D_MODEL)),
            _const_spec((1, D_MODEL)),
            _const_spec((D_MODEL, LANES)),
            _const_spec((1, LANES)),
        ],
        out_specs=[row_spec, row_spec, pl.BlockSpec((tm, LANES), lambda i: (i, 0)),
                   pl.BlockSpec((8, LANES), lambda i: (i, 0))],
        out_shape=[jax.ShapeDtypeStruct((t, D_MODEL), F32), jax.ShapeDtypeStruct((t, D_MODEL), F32),
                   jax.ShapeDtypeStruct((t, LANES), F32), jax.ShapeDtypeStruct((t // tm * 8, LANES), F32)],
        scratch_shapes=[pltpu.VMEM((tm, D_MODEL), BF16), pltpu.VMEM((tm, GM_WIDTH), F32),
                        pltpu.VMEM((tm, GM_WIDTH), BF16)],
        compiler_params=_cparams("parallel"),
        name="gmlp_mixer",
    )(x, mix_g.reshape(1, -1), w_in.astype(BF16), v_gain.reshape(1, -1), w_s, b_s.T,
      w_out.astype(BF16), ffn_g.reshape(1, -1), wr, br)


def _rank_kernel(route_ref, start_ref, dest_ref, carry_s):
    tm = route_ref.shape[0]

    @pl.when(pl.program_id(0) == 0)
    def _():
        carry_s[...] = jnp.zeros_like(carry_s)

    r = route_ref[...]
    lane = lax.broadcasted_iota(I32, (tm, LANES), 1)
    lanef = lane.astype(F32)
    oh0 = (lanef == r[:, R_E0:R_E0 + 1]).astype(F32)
    oh1 = (lanef == r[:, R_E1:R_E1 + 1]).astype(F32)
    both = oh0 + oh1
    row = lax.broadcasted_iota(I32, (tm, tm), 0)
    col = lax.broadcasted_iota(I32, (tm, tm), 1)
    strict_lower = (row > col).astype(BF16)
    base = start_ref[...] + carry_s[0:1, :] + _dot(strict_lower, both.astype(BF16))
    d0 = jnp.sum(oh0 * base, axis=-1, keepdims=True)
    d1 = jnp.sum(oh1 * (base + oh0), axis=-1, keepdims=True)
    cols = jnp.where(lane == 0, d0, jnp.where(lane == 1, d1, 0.0))
    dest_ref[...] = cols.T[0:8, :].astype(I32)
    carry_s[0:1, :] = carry_s[0:1, :] + jnp.sum(both, axis=0, keepdims=True)


def _moe_rank(route, start):
    t = route.shape[0]
    tm = min(TM_RANK, t)
    nt = t // tm
    dest = pl.pallas_call(
        _rank_kernel,
        grid=(nt,),
        in_specs=[pl.BlockSpec((tm, LANES), lambda i: (i, 0)), _const_spec((1, LANES))],
        out_specs=pl.BlockSpec((8, tm), lambda i: (i, 0)),
        out_shape=jax.ShapeDtypeStruct((nt * 8, tm), I32),
        scratch_shapes=[pltpu.VMEM((8, LANES), F32)],
        compiler_params=_cparams("arbitrary"),
        name="moe_rank",
    )(route, start)
    return dest.reshape(nt, 8, tm)[:, :2, :]


def _row_copy(src_ref, dst_ref, sem, s, d):
    return pltpu.make_async_copy(src_ref.at[pl.ds(s, 1)], dst_ref.at[pl.ds(d, 1)], sem)


_DISPATCH_BUFS = 3


def _dispatch_kernel(dest_ref, xn_ref, bufx_ref, x_s, load_sem, row_sem):
    tm = TM_TOK
    i = pl.program_id(0)
    n = pl.num_programs(0)
    slot = lax.rem(i, _DISPATCH_BUFS)
    nxt = lax.rem(i + 1, _DISPATCH_BUFS)

    def load(tile, s):
        return pltpu.make_async_copy(xn_ref.at[pl.ds(tile * tm, tm)], x_s.at[s], load_sem.at[s])

    def drain_rows(s):
        for _ in range(2 * tm):
            _row_copy(x_s.at[s], bufx_ref, row_sem.at[s], 0, 0).wait()

    @pl.when(i == 0)
    def _():
        load(0, 0).start()

    @pl.when(i >= 2)
    def _():
        drain_rows(nxt)

    @pl.when(i + 1 < n)
    def _():
        load(i + 1, nxt).start()

    load(i, slot).wait()
    for t in range(tm):
        for k in range(2):
            _row_copy(x_s.at[slot], bufx_ref, row_sem.at[slot], t, dest_ref[k * tm + t]).start(priority=k)

    @pl.when(i == n - 1)
    def _():
        drain_rows(slot)

    @pl.when((i == n - 1) & (i >= 1))
    def _():
        drain_rows(lax.rem(i + 2, _DISPATCH_BUFS))


def _moe_dispatch(xn, dest_flat):
    t = xn.shape[0]
    tm = TM_TOK
    return pl.pallas_call(
        _dispatch_kernel,
        grid=(t // tm,),
        in_specs=[pl.BlockSpec((2 * tm,), lambda i: (i,), memory_space=pltpu.SMEM),
                  pl.BlockSpec(memory_space=pl.ANY)],
        out_specs=pl.BlockSpec(memory_space=pl.ANY),
        out_shape=jax.ShapeDtypeStruct((2 * t, D_MODEL), F32),
        scratch_shapes=[pltpu.VMEM((_DISPATCH_BUFS, tm, D_MODEL), F32),
                        pltpu.SemaphoreType.DMA((_DISPATCH_BUFS,)),
                        pltpu.SemaphoreType.DMA((_DISPATCH_BUFS,))],
        compiler_params=_cparams("arbitrary"),
        name="moe_dispatch",
    )(dest_flat, xn)


def _expert_kernel(ie_ref, ib_ref, ilo_ref, ihi_ref, ifirst_ref, n_ref,
                   x_ref, w1_ref, w3_ref, w2_ref, y_ref, w1_s, w3_s, w2_s):
    w = pl.program_id(0)
    live = w < n_ref[0]
    e = ie_ref[w]
    prev = ie_ref[jnp.maximum(w - 1, 0)]

    @pl.when(live & ((w == 0) | (e != prev)))
    def _():
        w1_s[...] = w1_ref[0].astype(BF16)
        w3_s[...] = w3_ref[0].astype(BF16)
        w2_s[...] = w2_ref[0].astype(BF16)

    def ffn():
        rows = lax.broadcasted_iota(I32, (x_ref.shape[0], 1), 0)
        mine = (rows >= ilo_ref[w]) & (rows < ihi_ref[w])
        x = jnp.where(mine, x_ref[...], 0.0).astype(BF16)
        a = _dot(x, w1_s[...])
        b = _dot(x, w3_s[...])
        mid = (a * jax.nn.sigmoid(a) * b).astype(BF16)
        return _dot(mid, w2_s[...])

    @pl.when(live & (ifirst_ref[w] == 1))
    def _():
        y_ref[...] = ffn()

    @pl.when(live & (ifirst_ref[w] == 0))
    def _():
        y_ref[...] += ffn()


def _moe_experts(bufx, items, w1, w3, w2, layer):
    n_rows = bufx.shape[0]
    br = MOE_ROWS
    n_items_max = items[0].shape[0]

    def row_map(w, ie, ib, ilo, ihi, ifirst, n):
        return (ib[w], 0)

    def w_map(w, ie, ib, ilo, ihi, ifirst, n):
        return (layer, ie[w], 0, 0)

    grid_spec = pltpu.PrefetchScalarGridSpec(
        num_scalar_prefetch=6,
        grid=(n_items_max,),
        in_specs=[pl.BlockSpec((br, D_MODEL), row_map),
                  pl.BlockSpec((None, 1, D_MODEL, D_EXPERT), w_map),
                  pl.BlockSpec((None, 1, D_MODEL, D_EXPERT), w_map),
                  pl.BlockSpec((None, 1, D_EXPERT, D_MODEL), w_map)],
        out_specs=pl.BlockSpec((br, D_MODEL), row_map),
        scratch_shapes=[pltpu.VMEM((D_MODEL, D_EXPERT), BF16), pltpu.VMEM((D_MODEL, D_EXPERT), BF16),
                        pltpu.VMEM((D_EXPERT, D_MODEL), BF16)],
    )
    return pl.pallas_call(
        _expert_kernel,
        grid_spec=grid_spec,
        out_shape=jax.ShapeDtypeStruct((n_rows, D_MODEL), F32),
        compiler_params=_cparams("arbitrary"),
        name="moe_experts",
    )(*items, bufx, w1, w3, w2)


def _moe_plan(counts, t):
    br = MOE_ROWS
    nb = 2 * t // br
    n_items_max = nb + N_EXPERTS - 1
    cnt = counts[:N_EXPERTS].astype(I32)
    end = jnp.cumsum(cnt)
    start = end - cnt

    first_blk = start // br
    n_blk = jnp.where(cnt > 0, (end - 1) // br - first_blk + 1, 0)
    iend = jnp.cumsum(n_blk)
    istart = iend - n_blk
    n_items = iend[-1:]
    w = jnp.minimum(jnp.arange(n_items_max, dtype=I32), n_items[0] - 1)
    ie = jnp.minimum(jnp.sum((iend[None, :] <= w[:, None]).astype(I32), axis=1), N_EXPERTS - 1)
    ib = first_blk[ie] + (w - istart[ie])
    ilo = jnp.maximum(start[ie], ib * br) - ib * br
    ihi = jnp.minimum(end[ie], (ib + 1) * br) - ib * br
    ifirst = jnp.concatenate([jnp.ones((1,), I32), (ib[1:] != ib[:-1]).astype(I32)])
    items = tuple(a.astype(I32) for a in (ie, ib, ilo, ihi, ifirst, n_items))
    return items


def _moe(xn, route, tile_counts, w1, w3, w2, layer):
    t = xn.shape[0]
    counts = jnp.sum(tile_counts.reshape(-1, 8, LANES)[:, 0, :], axis=0)
    start = (jnp.cumsum(counts) - counts).reshape(1, LANES)
    dest = _moe_rank(route, start)
    nt_r, _, tm_r = dest.shape
    dest = dest.reshape(nt_r, 2, tm_r // TM_TOK, TM_TOK).transpose(0, 2, 1, 3).reshape(-1)
    items = _moe_plan(counts, t)
    bufx = _moe_dispatch(xn, dest)
    bufy = _moe_experts(bufx, items, w1, w3, w2, layer)
    return bufy, dest


def _ple_kernel(dest_ref, dest_next_ref, h_ref, route_ref, p_ref, bufy_ref, pleg_ref, wg_ref, wp_ref,
                fing_ref, out_ref, y_s, sem, *, final):
    tm = h_ref.shape[0]
    i = pl.program_id(0)
    last = pl.num_programs(0) - 1
    slot = lax.rem(i, 2)

    def gather(dref, s):
        for t in range(tm):
            for k in range(2):
                _row_copy(bufy_ref, y_s.at[s, k], sem.at[s], dref[k * tm + t], t).start(priority=k)

    def drain(s):
        for _ in range(2 * tm):
            _row_copy(bufy_ref, y_s.at[s, 0], sem.at[s], 0, 0).wait()

    @pl.when(i == 0)
    def _():
        gather(dest_ref, 0)

    drain(slot)
    gather(dest_next_ref, 1 - slot)

    r = route_ref[...]
    h = h_ref[...] + (y_s[slot, 0] * r[:, R_G0:R_G0 + 1] + y_s[slot, 1] * r[:, R_G1:R_G1 + 1])
    gate = jax.nn.sigmoid(_dot(_rms(h, pleg_ref[...]).astype(BF16), wg_ref[...]))
    h = h + gate * _dot(p_ref[...].astype(BF16), wp_ref[...])
    out_ref[...] = _rms(h, fing_ref[...]) if final else h

    @pl.when(i == last)
    def _():
        drain(1 - slot)


def _moe_combine_ple(h, route, bufy, dest, p_all, layer, ple_g, w_gate, w_proj, final_g, final):
    t = h.shape[0]
    tm = TM_TOK
    nt = t // tm
    row_spec = pl.BlockSpec((tm, D_MODEL), lambda i: (i, 0))
    return pl.pallas_call(
        functools.partial(_ple_kernel, final=final),
        grid=(nt,),
        in_specs=[pl.BlockSpec((2 * tm,), lambda i: (i,), memory_space=pltpu.SMEM),
                  pl.BlockSpec((2 * tm,), lambda i: (jnp.minimum(i + 1, nt - 1),), memory_space=pltpu.SMEM),
                  row_spec,
                  pl.BlockSpec((tm, LANES), lambda i: (i, 0)),
                  pl.BlockSpec((None, tm, PLE_DIM), lambda i: (layer, i, 0)),
                  pl.BlockSpec(memory_space=pl.ANY),
                  _const_spec((1, D_MODEL)),
                  _const_spec((D_MODEL, D_MODEL)),
                  _const_spec((PLE_DIM, D_MODEL)),
                  _const_spec((1, D_MODEL))],
        out_specs=row_spec,
        out_shape=jax.ShapeDtypeStruct((t, D_MODEL), F32),
        scratch_shapes=[pltpu.VMEM((2, 2, tm, D_MODEL), F32), pltpu.SemaphoreType.DMA((2,))],
        compiler_params=_cparams("arbitrary"),
        name="moe_combine_ple",
    )(dest, dest, h, route, p_all, bufy, ple_g.reshape(1, -1), w_gate.astype(BF16), w_proj.astype(BF16),
      final_g.reshape(1, -1))


_P_Q = 0
_P_K = _P_Q + N_HEADS * HEAD_DIM
_P_V = _P_K + KV_DIM
_P_QI = _P_V + KV_DIM
_P_KI = _P_QI + IDX_HEADS * LANES
_P_WI = _P_KI + LANES
_P_END = _P_WI + LANES


def _rope_tables(pos, inv_ref, half):
    tm = pos.shape[0]
    lane = lax.broadcasted_iota(I32, (tm, LANES), 1)
    ang = pos * inv_ref[...]
    rot = lane < 2 * half
    cos = jnp.where(rot, jnp.cos(ang), 1.0)
    sin = jnp.sin(ang)
    s_lo = jnp.where(lane < half, -sin, 0.0)
    s_hi = jnp.where(rot & (lane >= half), sin, 0.0)
    return cos, s_lo, s_hi


def _rope(x, tables, half):
    cos, s_lo, s_hi = tables
    return x * cos + pltpu.roll(x, LANES - half, 1) * s_lo + pltpu.roll(x, half, 1) * s_hi


def _dsa_in_kernel(h_ref, g_ref, pos_ref, w_ref, invq_ref, invi_ref,
                   q_ref, k_ref, v_ref, qi_ref, ki_ref, wi_ref, hn_s):
    hn_s[...] = _rms(h_ref[...], g_ref[...]).astype(BF16)
    pos = pos_ref[...].astype(F32)
    tq = _rope_tables(pos, invq_ref, ROT_DIM // 2)
    ti = _rope_tables(pos, invi_ref, IDX_ROT_DIM // 2)

    def proj(c0, width):
        return _dot(hn_s[...], w_ref[:, c0:c0 + width])

    def rope_blocks(z, out_ref, c_out, tables, half):
        for b in range(z.shape[1] // LANES):
            blk = _rope(z[:, b * LANES:(b + 1) * LANES], tables, half)
            out_ref[:, c_out + b * LANES:c_out + (b + 1) * LANES] = blk.astype(BF16)

    nw = 512
    for c in range(0, N_HEADS * HEAD_DIM, nw):
        rope_blocks(proj(_P_Q + c, nw), q_ref, c, tq, ROT_DIM // 2)
    kv = proj(_P_K, 2 * KV_DIM)
    rope_blocks(kv[:, :KV_DIM], k_ref, 0, tq, ROT_DIM // 2)
    v_ref[...] = kv[:, KV_DIM:].astype(BF16)
    for c in range(0, IDX_HEADS * LANES, nw):
        rope_blocks(proj(_P_QI + c, nw), qi_ref, c, ti, IDX_ROT_DIM // 2)
    kw = proj(_P_KI, 2 * LANES)
    rope_blocks(kw[:, :LANES], ki_ref, 0, ti, IDX_ROT_DIM // 2)
    wi_ref[...] = kw[:, LANES:] * (IDX_HEADS ** -0.5 * IDX_DIM ** -0.5)


def _inv_freq_lanes(rot_dim):
    half = rot_dim // 2
    inv = ROPE_THETA ** (-jnp.arange(0, rot_dim, 2, dtype=F32) / rot_dim)
    lanes = jnp.zeros((LANES,), F32).at[:rot_dim].set(jnp.concatenate([inv, inv]))
    del half
    return lanes.reshape(1, LANES)


def _dsa_in_weights(w_in):
    d = w_in.shape[0]
    c_q, c_k, c_v = N_HEADS * HEAD_DIM, KV_DIM, KV_DIM
    c_qi, c_ki = IDX_HEADS * IDX_DIM, IDX_DIM
    o = np.cumsum([0, c_q, c_k, c_v, c_qi, c_ki, IDX_HEADS])
    wq, wk, wv, wqi, wki, wwi = (w_in[:, o[i]:o[i + 1]] for i in range(6))
    wqi = jnp.pad(wqi.reshape(d, IDX_HEADS, IDX_DIM), ((0, 0), (0, 0), (0, LANES - IDX_DIM)))
    wki = jnp.pad(wki, ((0, 0), (0, LANES - IDX_DIM)))
    wwi = jnp.pad(wwi, ((0, 0), (0, LANES - IDX_HEADS)))
    return jnp.concatenate([wq, wk, wv, wqi.reshape(d, IDX_HEADS * LANES), wki, wwi], axis=1).astype(BF16)


def _dsa_in(h, mix_g, positions, w_in):
    t = h.shape[0]
    tm = min(TM_MIX, t)
    spec = lambda w: pl.BlockSpec((tm, w), lambda i: (i, 0))
    widths = (N_HEADS * HEAD_DIM, KV_DIM, KV_DIM, IDX_HEADS * LANES, LANES)
    return pl.pallas_call(
        _dsa_in_kernel,
        grid=(t // tm,),
        in_specs=[spec(D_MODEL), _const_spec((1, D_MODEL)), spec(1), _const_spec((D_MODEL, _P_END)),
                  _const_spec((1, LANES)), _const_spec((1, LANES))],
        out_specs=[spec(w) for w in widths] + [spec(LANES)],
        out_shape=[jax.ShapeDtypeStruct((t, w), BF16) for w in widths]
        + [jax.ShapeDtypeStruct((t, LANES), F32)],
        scratch_shapes=[pltpu.VMEM((tm, D_MODEL), BF16)],
        compiler_params=_cparams("parallel"),
        name="dsa_in_proj",
    )(h, mix_g.reshape(1, -1), positions.reshape(t, 1), _dsa_in_weights(w_in),
      _inv_freq_lanes(ROT_DIM), _inv_freq_lanes(IDX_ROT_DIM))


_INT_MIN = -2 ** 31
_U_NEG_INF = 0x007FFFFF


def _key_to_f32(u):
    bits = jnp.where(u < 0, u ^ _INT_MIN, ~u)
    return lax.bitcast_convert_type(bits, F32)


def _dsa_core_kernel(q_ref, qi_ref, wi_ref, k_ref, v_ref, ki_ref, o_ref,
                     score_s, bias_s, lg_s, mx_s, l_s, acc_s, *, k_top):
    qb = q_ref.shape[0]
    sk = k_ref.shape[0]
    n_fold = K_TILE // LANES
    q_end = (pl.program_id(1) + 1) * qb
    n_kt = (q_end + (K_TILE - 1)) // K_TILE
    qpos = q_end - qb + lax.broadcasted_iota(I32, (qb, 1), 0)
    neg_inf = jnp.float32(-jnp.inf)

    def tile_start(kt):
        return pl.multiple_of(kt * K_TILE, K_TILE)

    def chunks(ref, k0):
        tile = ref[:, pl.ds(k0, K_TILE)]
        return [tile[:, c * LANES:(c + 1) * LANES] for c in range(n_fold)]

    wcols = [wi_ref[:, hd:hd + 1] for hd in range(IDX_HEADS)]

    def score_tile(kt, c):
        k0 = tile_start(kt)
        kit = ki_ref[pl.ds(k0, K_TILE), :]
        acc = jnp.zeros((qb, K_TILE), F32)
        for hd in range(IDX_HEADS):
            rel = jnp.maximum(_dot_nt(qi_ref[:, hd * LANES:(hd + 1) * LANES], kit), 0.0)
            acc = acc + wcols[hd] * rel
        kpos = k0 + lax.broadcasted_iota(I32, (qb, K_TILE), 1)
        score_s[:, pl.ds(k0, K_TILE)] = jnp.where(kpos <= qpos, acc, neg_inf)
        return c

    lax.fori_loop(0, n_kt, score_tile, 0)

    def count_ge(thr):
        def body(kt, acc):
            for s in chunks(score_s, tile_start(kt)):
                acc = acc + jnp.where(s >= thr, 1.0, 0.0)
            return acc

        acc = lax.fori_loop(0, n_kt, body, jnp.zeros((qb, LANES), F32))
        return jnp.sum(acc, axis=-1, keepdims=True)

    kf = jnp.float32(k_top)

    def bisect(p, ans):
        bit = lax.shift_left(jnp.int32(1), 31 - p)
        cand = ans | bit
        force = (cand >= 0) & (cand <= _U_NEG_INF)
        ok = (count_ge(_key_to_f32(cand)) >= kf) | force
        return jnp.where(ok, cand, ans)

    ans = lax.fori_loop(0, 32, bisect, jnp.zeros((qb, 1), I32))
    thr = _key_to_f32(jnp.where(ans == _U_NEG_INF, ans + 1, ans))
    thr_up = _key_to_f32(ans + 1)

    c_hi = count_ge(thr_up)
    need0 = kf - c_hi
    rem0 = count_ge(thr) - c_hi
    def init_bias(kt, c):
        k0 = tile_start(kt)
        bias_s[:, pl.ds(k0, K_TILE)] = jnp.where(score_s[:, pl.ds(k0, K_TILE)] >= thr_up, 0.0, NEG_BIG)
        return c

    lax.fori_loop(0, n_kt, init_bias, 0)

    def active_of(need, rem):
        return (need > 0.0) & (rem > 0.0)

    def bin_cond(carry):
        return carry[2] > 0.0

    def bin_body(carry):
        need, rem, _ = carry
        active = active_of(need, rem)
        lane = lax.broadcasted_iota(I32, (qb, LANES), 1)

        def in_bin(s, b):
            return (s >= thr) & (b != 0.0)

        def best_of(kt, best):
            k0 = tile_start(kt)
            for s, b in zip(chunks(score_s, k0), chunks(bias_s, k0)):
                best = jnp.maximum(best, jnp.where(in_bin(s, b), s, neg_inf))
            return best

        best = lax.fori_loop(0, n_kt, best_of, jnp.full((qb, LANES), neg_inf, F32))
        m = jnp.max(best, axis=-1, keepdims=True)

        def first_of(kt, first):
            k0 = tile_start(kt)
            for c, (s, b) in enumerate(zip(chunks(score_s, k0), chunks(bias_s, k0))):
                hit = (s == m) & in_bin(s, b)
                first = jnp.minimum(first, jnp.where(hit, lane + (k0 + c * LANES), sk))
            return first

        first = lax.fori_loop(0, n_kt, first_of, jnp.full((qb, LANES), sk, I32))
        pick = jnp.min(first, axis=-1, keepdims=True)

        def mark(kt, c):
            k0 = tile_start(kt)
            kpos = k0 + lax.broadcasted_iota(I32, (qb, K_TILE), 1)
            b = bias_s[:, pl.ds(k0, K_TILE)]
            bias_s[:, pl.ds(k0, K_TILE)] = jnp.where(active & (kpos == pick), 0.0, b)
            return c

        lax.fori_loop(0, n_kt, mark, 0)
        step = jnp.where(active, 1.0, 0.0)
        need, rem = need - step, rem - step
        go = jnp.max(jnp.where(active_of(need, rem), 1.0, 0.0))
        return need, rem, go

    go0 = jnp.max(jnp.where(active_of(need0, rem0), 1.0, 0.0))
    lax.while_loop(bin_cond, bin_body, (need0, rem0, go0))

    scale = HEAD_DIM ** -0.5

    def q_rows(g):
        return jnp.concatenate(
            [q_ref[:, (g * HEADS_PER_KV + r) * HEAD_DIM:(g * HEADS_PER_KV + r + 1) * HEAD_DIM]
             for r in range(HEADS_PER_KV)], axis=0)

    def fold(x, op):
        out = x[:, 0:LANES]
        for c in range(1, n_fold):
            out = op(out, x[:, c * LANES:(c + 1) * LANES])
        return out

    mx_s[...] = jnp.full(mx_s.shape, NEG_BIG, F32)
    l_s[...] = jnp.zeros(l_s.shape, F32)
    acc_s[...] = jnp.zeros(acc_s.shape, F32)

    def logit_tile(kt, c):
        k0 = tile_start(kt)
        bias = jnp.concatenate([bias_s[:, pl.ds(k0, K_TILE)]] * HEADS_PER_KV, axis=0)
        for g in range(N_KV_HEADS):
            kt_ = k_ref[pl.ds(k0, K_TILE), g * HEAD_DIM:(g + 1) * HEAD_DIM]
            logits = _dot_nt(q_rows(g), kt_) * scale + bias
            lg_s[g, :, pl.ds(k0, K_TILE)] = logits
            mx_s[g] = jnp.maximum(mx_s[g], fold(logits, jnp.maximum))
        return c

    lax.fori_loop(0, n_kt, logit_tile, 0)
    row_max = [jnp.max(mx_s[g], axis=-1, keepdims=True) for g in range(N_KV_HEADS)]

    def att_tile(kt, c):
        k0 = tile_start(kt)
        for g in range(N_KV_HEADS):
            vt_ = v_ref[pl.ds(k0, K_TILE), g * HEAD_DIM:(g + 1) * HEAD_DIM]
            pexp = jnp.exp(lg_s[g, :, pl.ds(k0, K_TILE)] - row_max[g])
            l_s[g] = l_s[g] + fold(pexp, jnp.add)
            acc_s[g] = acc_s[g] + _dot(pexp.astype(BF16), vt_)
        return c

    lax.fori_loop(0, n_kt, att_tile, 0)
    for g in range(N_KV_HEADS):
        og = acc_s[g] / jnp.sum(l_s[g], axis=-1, keepdims=True)
        for r in range(HEADS_PER_KV):
            c0 = (g * HEADS_PER_KV + r) * HEAD_DIM
            o_ref[:, c0:c0 + HEAD_DIM] = og[r * qb:(r + 1) * qb, :].astype(BF16)


def _dsa_core(q, k, v, qi, ki, wi, batch, seq):
    qb = Q_BLOCK
    k_top = min(TOPK_MAX, seq // 4)
    view = lambda a: a.reshape(batch, seq, a.shape[-1])
    q3, k3, v3, qi3, ki3, wi3 = (view(a) for a in (q, k, v, qi, ki, wi))
    qspec = lambda w: pl.BlockSpec((None, qb, w), lambda b, j: (b, j, 0))
    kspec = lambda w: pl.BlockSpec((None, seq, w), lambda b, j: (b, 0, 0))
    rows = HEADS_PER_KV * qb
    out = pl.pallas_call(
        functools.partial(_dsa_core_kernel, k_top=k_top),
        grid=(batch, seq // qb),
        in_specs=[qspec(D_MODEL), qspec(IDX_HEADS * LANES), qspec(LANES),
                  kspec(KV_DIM), kspec(KV_DIM), kspec(LANES)],
        out_specs=qspec(D_MODEL),
        out_shape=jax.ShapeDtypeStruct((batch, seq, D_MODEL), BF16),
        scratch_shapes=[pltpu.VMEM((qb, seq), F32), pltpu.VMEM((qb, seq), F32),
                        pltpu.VMEM((N_KV_HEADS, rows, seq), F32),
                        pltpu.VMEM((N_KV_HEADS, rows, LANES), F32),
                        pltpu.VMEM((N_KV_HEADS, rows, LANES), F32),
                        pltpu.VMEM((N_KV_HEADS, rows, HEAD_DIM), F32)],
        compiler_params=_cparams("parallel", "arbitrary"),
        name="dsa_core",
    )(q3, qi3, wi3, k3, v3, ki3)
    return out.reshape(batch * seq, D_MODEL)


def _dsa_out_kernel(h_ref, a_ref, wo_ref, ffng_ref, wr_ref, br_ref, h_out_ref, xn_ref, route_ref,
                    cnt_ref):
    h = h_ref[...] + _dot(a_ref[...], wo_ref[...])
    h_out_ref[...] = h
    xn = _rms(h, ffng_ref[...])
    xn_ref[...] = xn
    route_ref[...], cnt_ref[...] = _route(xn, wr_ref, br_ref)


def _dsa_out(h, att, w_out, ffn_g, wr, br):
    t = h.shape[0]
    tm = min(TM_MIX, t)
    row_spec = pl.BlockSpec((tm, D_MODEL), lambda i: (i, 0))
    return pl.pallas_call(
        _dsa_out_kernel,
        grid=(t // tm,),
        in_specs=[row_spec, row_spec, _const_spec((D_MODEL, D_MODEL)), _const_spec((1, D_MODEL)),
                  _const_spec((D_MODEL, LANES)), _const_spec((1, LANES))],
        out_specs=[row_spec, row_spec, pl.BlockSpec((tm, LANES), lambda i: (i, 0)),
                   pl.BlockSpec((8, LANES), lambda i: (i, 0))],
        out_shape=[jax.ShapeDtypeStruct((t, D_MODEL), F32), jax.ShapeDtypeStruct((t, D_MODEL), F32),
                   jax.ShapeDtypeStruct((t, LANES), F32), jax.ShapeDtypeStruct((t // tm * 8, LANES), F32)],
        compiler_params=_cparams("parallel"),
        name="dsa_out_proj",
    )(h, att, w_out.astype(BF16), ffn_g.reshape(1, -1), wr, br)


def kernel(x, p, positions, mix_norm, a_w_in, a_v_gain, a_w_s, a_b_s, a_w_out, b_w_in, b_w_out,
           ffn_norm, moe_w_group, moe_b_group, moe_w_expert, moe_b_expert, moe_w1, moe_w3, moe_w2,
           ple_norm, ple_w_gate, ple_w_proj, final_norm):
    batch, seq, d = x.shape
    depth = mix_norm.shape[0]
    t = batch * seq
    h = x.reshape(t, d)
    for i in range(depth):
        j = i // 2
        wr, br = _router_weights(moe_w_group[i], moe_b_group[i], moe_w_expert[i], moe_b_expert[i])
        if i % 2 == 0:
            h, xn, route, tile_counts = _gmlp_layer(h, mix_norm[i], a_w_in[j], a_v_gain[j], a_w_s[j],
                                                    a_b_s[j], a_w_out[j], ffn_norm[i], wr, br)
        else:
            q, k, v, qi, ki, wi = _dsa_in(h, mix_norm[i], positions, b_w_in[j])
            att = _dsa_core(q, k, v, qi, ki, wi, batch, seq)
            h, xn, route, tile_counts = _dsa_out(h, att, b_w_out[j], ffn_norm[i], wr, br)
        bufy, dest = _moe(xn, route, tile_counts, moe_w1, moe_w3, moe_w2, i)
        h = _moe_combine_ple(h, route, bufy, dest, p.reshape(depth, t, -1), i, ple_norm[i],
                             ple_w_gate[i], ple_w_proj[i], final_norm, final=(i == depth - 1))
    return h.reshape(batch, seq, d)
```

```python
import functools

import numpy as np
import jax
import jax.numpy as jnp
from jax import lax
from jax.experimental import pallas as pl
from jax.experimental.pallas import tpu as pltpu

F32 = jnp.float32
BF16 = jnp.bfloat16
I32 = jnp.int32

D_MODEL = 1024
EPS = 1e-6
ROPE_THETA = 500000.0
GM_CHUNK = 128
GM_WIDTH = 2 * D_MODEL
GM_GROUPS = 8
GM_GROUP_DIM = GM_WIDTH // GM_GROUPS
N_HEADS = 8
HEAD_DIM = D_MODEL // N_HEADS
N_KV_HEADS = 2
HEADS_PER_KV = N_HEADS // N_KV_HEADS
KV_DIM = N_KV_HEADS * HEAD_DIM
ROT_DIM = HEAD_DIM // 4
IDX_HEADS = 8
IDX_DIM = 64
IDX_ROT_DIM = IDX_DIM // 4
TOPK_MAX = 256
N_EXPERT_GROUPS = 4
EXPERTS_PER_GROUP = 8
N_EXPERTS = N_EXPERT_GROUPS * EXPERTS_PER_GROUP
D_EXPERT = D_MODEL // 2
PLE_DIM = 256

LANES = 128
V7X_VMEM_LIMIT = 56 * 1024 * 1024

TM_MIX = 256
TM_TOK = 256
TM_RANK = 512
MOE_ROWS = 256
Q_BLOCK = 128
K_TILE = 512
NEG_BIG = -1e30

R_E0, R_E1, R_G0, R_G1 = 0, 1, 2, 3
R_EXP0 = N_EXPERT_GROUPS


def _cparams(*sem):
    return pltpu.CompilerParams(dimension_semantics=sem, vmem_limit_bytes=V7X_VMEM_LIMIT)


def _const_spec(shape):
    nd = len(shape)
    return pl.BlockSpec(shape, lambda *_: (0,) * nd, pipeline_mode=pl.Buffered(1))


def _rms(x, g):
    return x * lax.rsqrt(jnp.mean(x * x, axis=-1, keepdims=True) + EPS) * g


def _gelu(x):
    return jax.nn.gelu(x)


def _dot(a, b):
    return jnp.dot(a, b, preferred_element_type=F32)


def _dot_nt(a, b):
    return lax.dot_general(a, b, (((1,), (1,)), ((), ())), preferred_element_type=F32)


def _split_bf16(a):
    hi = a.astype(BF16)
    lo = (a - hi.astype(F32)).astype(BF16)
    return hi, lo


def _dot_3pass(a, b):
    ah, al = _split_bf16(a)
    bh, bl = _split_bf16(b)
    return _dot(ah, bh) + (_dot(al, bh) + _dot(ah, bl))


def _route(xn, wr_ref, br_ref):
    tm = xn.shape[0]
    logits = _dot_3pass(xn, wr_ref[...]) + br_ref[...]
    lane = lax.broadcasted_iota(I32, (tm, LANES), 1)
    neg_inf = jnp.float32(-jnp.inf)

    gmask = lane < N_EXPERT_GROUPS
    gl = jnp.where(gmask, logits, neg_inf)
    gmax = jnp.max(gl, axis=-1, keepdims=True)
    g_sel = jnp.min(jnp.where(gl == gmax, lane, LANES), axis=-1, keepdims=True)
    p_group = 1.0 / jnp.sum(jnp.where(gmask, jnp.exp(gl - gmax), 0.0), axis=-1, keepdims=True)

    lo = R_EXP0 + EXPERTS_PER_GROUP * g_sel
    lmask = (lane >= lo) & (lane < lo + EXPERTS_PER_GROUP)
    ll = jnp.where(lmask, logits, neg_inf)
    lmax = jnp.max(ll, axis=-1, keepdims=True)
    le = jnp.where(lmask, jnp.exp(ll - lmax), 0.0)
    probs = jnp.where(lmask, le / jnp.sum(le, axis=-1, keepdims=True), -1.0)
    p1 = jnp.max(probs, axis=-1, keepdims=True)
    i1 = jnp.min(jnp.where(probs == p1, lane, LANES), axis=-1, keepdims=True)
    probs2 = jnp.where(lane == i1, -1.0, probs)
    p2 = jnp.max(probs2, axis=-1, keepdims=True)
    i2 = jnp.min(jnp.where(probs2 == p2, lane, LANES), axis=-1, keepdims=True)
    denom = p1 + p2
    g1 = p_group * p1 / denom
    g2 = p_group * p2 / denom
    e1 = (i1 - R_EXP0).astype(F32)
    e2 = (i2 - R_EXP0).astype(F32)
    out = jnp.where(lane == R_E0, e1, 0.0)
    out = jnp.where(lane == R_E1, e2, out)
    out = jnp.where(lane == R_G0, g1, out)
    out = jnp.where(lane == R_G1, g2, out)
    hits = jnp.where((lane == i1 - R_EXP0) | (lane == i2 - R_EXP0), 1.0, 0.0)
    counts = jnp.where(lax.broadcasted_iota(I32, (8, LANES), 0) == 0,
                       jnp.sum(hits, axis=0, keepdims=True), 0.0)
    return out, counts


def _router_weights(w_group, b_group, w_expert, b_expert):
    wr = jnp.zeros((D_MODEL, LANES), F32)
    wr = wr.at[:, :N_EXPERT_GROUPS].set(w_group).at[:, R_EXP0:R_EXP0 + N_EXPERTS].set(w_expert)
    br = jnp.zeros((1, LANES), F32)
    br = br.at[0, :N_EXPERT_GROUPS].set(b_group).at[0, R_EXP0:R_EXP0 + N_EXPERTS].set(b_expert)
    return wr, br


def _gmlp_kernel(x_ref, mixg_ref, win_ref, vgain_ref, ws_ref, bst_ref, wout_ref, ffng_ref, wr_ref,
                 br_ref, h_ref, xn_ref, route_ref, cnt_ref, xn_s, v_s, prod_s):
    tm = x_ref.shape[0]
    xn_s[...] = _rms(x_ref[...], mixg_ref[...]).astype(BF16)

    nt = 512
    ssq = jnp.zeros((tm, 1), F32)
    for j in range(GM_WIDTH // nt):
        z = _gelu(_dot(xn_s[...], win_ref[:, GM_WIDTH + j * nt:GM_WIDTH + (j + 1) * nt]))
        v_s[:, j * nt:(j + 1) * nt] = z
        ssq = ssq + jnp.sum(z * z, axis=-1, keepdims=True)
    rinv = lax.rsqrt(ssq * (1.0 / GM_WIDTH) + EPS)

    row = lax.broadcasted_iota(I32, (GM_CHUNK, GM_CHUNK), 0)
    col = lax.broadcasted_iota(I32, (GM_CHUNK, GM_CHUNK), 1)
    tril = row >= col
    gd = GM_GROUP_DIM
    for g in range(GM_GROUPS):
        wm = jnp.where(tril, ws_ref[g], 0.0).astype(BF16)
        vg = (v_s[:, g * gd:(g + 1) * gd] * rinv * vgain_ref[:, g * gd:(g + 1) * gd]).astype(BF16)
        u = _gelu(_dot(xn_s[...], win_ref[:, g * gd:(g + 1) * gd]))
        bias = bst_ref[:, g:g + 1]
        for c in range(tm // GM_CHUNK):
            rows = slice(c * GM_CHUNK, (c + 1) * GM_CHUNK)
            sv = _dot(wm, vg[rows, :]) + bias
            prod_s[rows, g * gd:(g + 1) * gd] = (u[rows, :] * sv).astype(BF16)

    h = x_ref[...] + _dot(prod_s[...], wout_ref[...])
    h_ref[...] = h
    xn = _rms(h, ffng_ref[...])
    xn_ref[...] = xn
    route_ref[...], cnt_ref[...] = _route(xn, wr_ref, br_ref)


def _gmlp_layer(x, mix_g, w_in, v_gain, w_s, b_s, w_out, ffn_g, wr, br):
    t = x.shape[0]
    tm = TM_MIX
    row_spec = pl.BlockSpec((tm, D_MODEL), lambda i: (i, 0))
    return pl.pallas_call(
        _gmlp_kernel,
        grid=(t // tm,),
        in_specs=[
            row_spec,
            _const_spec((1, D_MODEL)),
            _const_spec((D_MODEL, 2 * GM_WIDTH)),
            _const_spec((1, GM_WIDTH)),
            _const_spec((GM_GROUPS, GM_CHUNK, GM_CHUNK)),
            _const_spec((GM_CHUNK, GM_GROUPS)),
            _const_spec((GM_WIDTH, D_MODEL)),
            _const_spec((1, D_MODEL)),
            _const_spec((D_MODEL, LANES)),
            _const_spec((1, LANES)),
        ],
        out_specs=[row_spec, row_spec, pl.BlockSpec((tm, LANES), lambda i: (i, 0)),
                   pl.BlockSpec((8, LANES), lambda i: (i, 0))],
        out_shape=[jax.ShapeDtypeStruct((t, D_MODEL), F32), jax.ShapeDtypeStruct((t, D_MODEL), F32),
                   jax.ShapeDtypeStruct((t, LANES), F32), jax.ShapeDtypeStruct((t // tm * 8, LANES), F32)],
        scratch_shapes=[pltpu.VMEM((tm, D_MODEL), BF16), pltpu.VMEM((tm, GM_WIDTH), F32),
                        pltpu.VMEM((tm, GM_WIDTH), BF16)],
        compiler_params=_cparams("parallel"),
        name="gmlp_mixer",
    )(x, mix_g.reshape(1, -1), w_in.astype(BF16), v_gain.reshape(1, -1), w_s, b_s.T,
      w_out.astype(BF16), ffn_g.reshape(1, -1), wr, br)


def _rank_kernel(route_ref, start_ref, dest_ref, carry_s):
    tm = route_ref.shape[0]

    @pl.when(pl.program_id(0) == 0)
    def _():
        carry_s[...] = jnp.zeros_like(carry_s)

    r = route_ref[...]
    lane = lax.broadcasted_iota(I32, (tm, LANES), 1)
    lanef = lane.astype(F32)
    oh0 = (lanef == r[:, R_E0:R_E0 + 1]).astype(F32)
    oh1 = (lanef == r[:, R_E1:R_E1 + 1]).astype(F32)
    both = oh0 + oh1
    row = lax.broadcasted_iota(I32, (tm, tm), 0)
    col = lax.broadcasted_iota(I32, (tm, tm), 1)
    strict_lower = (row > col).astype(BF16)
    base = start_ref[...] + carry_s[0:1, :] + _dot(strict_lower, both.astype(BF16))
    d0 = jnp.sum(oh0 * base, axis=-1, keepdims=True)
    d1 = jnp.sum(oh1 * (base + oh0), axis=-1, keepdims=True)
    cols = jnp.where(lane == 0, d0, jnp.where(lane == 1, d1, 0.0))
    dest_ref[...] = cols.T[0:8, :].astype(I32)
    carry_s[0:1, :] = carry_s[0:1, :] + jnp.sum(both, axis=0, keepdims=True)


def _moe_rank(route, start):
    t = route.shape[0]
    tm = min(TM_RANK, t)
    nt = t // tm
    dest = pl.pallas_call(
        _rank_kernel,
        grid=(nt,),
        in_specs=[pl.BlockSpec((tm, LANES), lambda i: (i, 0)), _const_spec((1, LANES))],
        out_specs=pl.BlockSpec((8, tm), lambda i: (i, 0)),
        out_shape=jax.ShapeDtypeStruct((nt * 8, tm), I32),
        scratch_shapes=[pltpu.VMEM((8, LANES), F32)],
        compiler_params=_cparams("arbitrary"),
        name="moe_rank",
    )(route, start)
    return dest.reshape(nt, 8, tm)[:, :2, :]


def _row_copy(src_ref, dst_ref, sem, s, d):
    return pltpu.make_async_copy(src_ref.at[pl.ds(s, 1)], dst_ref.at[pl.ds(d, 1)], sem)


_DISPATCH_BUFS = 3


def _dispatch_kernel(dest_ref, xn_ref, bufx_ref, x_s, load_sem, row_sem):
    tm = TM_TOK
    i = pl.program_id(0)
    n = pl.num_programs(0)
    slot = lax.rem(i, _DISPATCH_BUFS)
    nxt = lax.rem(i + 1, _DISPATCH_BUFS)

    def load(tile, s):
        return pltpu.make_async_copy(xn_ref.at[pl.ds(tile * tm, tm)], x_s.at[s], load_sem.at[s])

    def drain_rows(s):
        for _ in range(2 * tm):
            _row_copy(x_s.at[s], bufx_ref, row_sem.at[s], 0, 0).wait()

    @pl.when(i == 0)
    def _():
        load(0, 0).start()

    @pl.when(i >= 2)
    def _():
        drain_rows(nxt)

    @pl.when(i + 1 < n)
    def _():
        load(i + 1, nxt).start()

    load(i, slot).wait()
    for t in range(tm):
        for k in range(2):
            _row_copy(x_s.at[slot], bufx_ref, row_sem.at[slot], t, dest_ref[k * tm + t]).start(priority=k)

    @pl.when(i == n - 1)
    def _():
        drain_rows(slot)

    @pl.when((i == n - 1) & (i >= 1))
    def _():
        drain_rows(lax.rem(i + 2, _DISPATCH_BUFS))


def _moe_dispatch(xn, dest_flat):
    t = xn.shape[0]
    tm = TM_TOK
    return pl.pallas_call(
        _dispatch_kernel,
        grid=(t // tm,),
        in_specs=[pl.BlockSpec((2 * tm,), lambda i: (i,), memory_space=pltpu.SMEM),
                  pl.BlockSpec(memory_space=pl.ANY)],
        out_specs=pl.BlockSpec(memory_space=pl.ANY),
        out_shape=jax.ShapeDtypeStruct((2 * t, D_MODEL), F32),
        scratch_shapes=[pltpu.VMEM((_DISPATCH_BUFS, tm, D_MODEL), F32),
                        pltpu.SemaphoreType.DMA((_DISPATCH_BUFS,)),
                        pltpu.SemaphoreType.DMA((_DISPATCH_BUFS,))],
        compiler_params=_cparams("arbitrary"),
        name="moe_dispatch",
    )(dest_flat, xn)


def _expert_kernel(ie_ref, ib_ref, ilo_ref, ihi_ref, ifirst_ref, n_ref,
                   x_ref, w1_ref, w3_ref, w2_ref, y_ref, w1_s, w3_s, w2_s):
    w = pl.program_id(0)
    live = w < n_ref[0]
    e = ie_ref[w]
    prev = ie_ref[jnp.maximum(w - 1, 0)]

    @pl.when(live & ((w == 0) | (e != prev)))
    def _():
        w1_s[...] = w1_ref[0].astype(BF16)
        w3_s[...] = w3_ref[0].astype(BF16)
        w2_s[...] = w2_ref[0].astype(BF16)

    def ffn():
        rows = lax.broadcasted_iota(I32, (x_ref.shape[0], 1), 0)
        mine = (rows >= ilo_ref[w]) & (rows < ihi_ref[w])
        x = jnp.where(mine, x_ref[...], 0.0).astype(BF16)
        a = _dot(x, w1_s[...])
        b = _dot(x, w3_s[...])
        mid = (a * jax.nn.sigmoid(a) * b).astype(BF16)
        return _dot(mid, w2_s[...])

    @pl.when(live & (ifirst_ref[w] == 1))
    def _():
        y_ref[...] = ffn()

    @pl.when(live & (ifirst_ref[w] == 0))
    def _():
        y_ref[...] += ffn()


def _moe_experts(bufx, items, w1, w3, w2, layer):
    n_rows = bufx.shape[0]
    br = MOE_ROWS
    n_items_max = items[0].shape[0]

    def row_map(w, ie, ib, ilo, ihi, ifirst, n):
        return (ib[w], 0)

    def w_map(w, ie, ib, ilo, ihi, ifirst, n):
        return (layer, ie[w], 0, 0)

    grid_spec = pltpu.PrefetchScalarGridSpec(
        num_scalar_prefetch=6,
        grid=(n_items_max,),
        in_specs=[pl.BlockSpec((br, D_MODEL), row_map),
                  pl.BlockSpec((None, 1, D_MODEL, D_EXPERT), w_map),
                  pl.BlockSpec((None, 1, D_MODEL, D_EXPERT), w_map),
                  pl.BlockSpec((None, 1, D_EXPERT, D_MODEL), w_map)],
        out_specs=pl.BlockSpec((br, D_MODEL), row_map),
        scratch_shapes=[pltpu.VMEM((D_MODEL, D_EXPERT), BF16), pltpu.VMEM((D_MODEL, D_EXPERT), BF16),
                        pltpu.VMEM((D_EXPERT, D_MODEL), BF16)],
    )
    return pl.pallas_call(
        _expert_kernel,
        grid_spec=grid_spec,
        out_shape=jax.ShapeDtypeStruct((n_rows, D_MODEL), F32),
        compiler_params=_cparams("arbitrary"),
        name="moe_experts",
    )(*items, bufx, w1, w3, w2)


def _moe_plan(counts, t):
    br = MOE_ROWS
    nb = 2 * t // br
    n_items_max = nb + N_EXPERTS - 1
    cnt = counts[:N_EXPERTS].astype(I32)
    end = jnp.cumsum(cnt)
    start = end - cnt

    first_blk = start // br
    n_blk = jnp.where(cnt > 0, (end - 1) // br - first_blk + 1, 0)
    iend = jnp.cumsum(n_blk)
    istart = iend - n_blk
    n_items = iend[-1:]
    w = jnp.minimum(jnp.arange(n_items_max, dtype=I32), n_items[0] - 1)
    ie = jnp.minimum(jnp.sum((iend[None, :] <= w[:, None]).astype(I32), axis=1), N_EXPERTS - 1)
    ib = first_blk[ie] + (w - istart[ie])
    ilo = jnp.maximum(start[ie], ib * br) - ib * br
    ihi = jnp.minimum(end[ie], (ib + 1) * br) - ib * br
    ifirst = jnp.concatenate([jnp.ones((1,), I32), (ib[1:] != ib[:-1]).astype(I32)])
    items = tuple(a.astype(I32) for a in (ie, ib, ilo, ihi, ifirst, n_items))
    return items


def _moe(xn, route, tile_counts, w1, w3, w2, layer):
    t = xn.shape[0]
    counts = jnp.sum(tile_counts.reshape(-1, 8, LANES)[:, 0, :], axis=0)
    start = (jnp.cumsum(counts) - counts).reshape(1, LANES)
    dest = _moe_rank(route, start)
    nt_r, _, tm_r = dest.shape
    dest = dest.reshape(nt_r, 2, tm_r // TM_TOK, TM_TOK).transpose(0, 2, 1, 3).reshape(-1)
    items = _moe_plan(counts, t)
    bufx = _moe_dispatch(xn, dest)
    bufy = _moe_experts(bufx, items, w1, w3, w2, layer)
    return bufy, dest


def _ple_kernel(dest_ref, dest_next_ref, h_ref, route_ref, p_ref, bufy_ref, pleg_ref, wg_ref, wp_ref,
                fing_ref, out_ref, y_s, sem, *, final):
    tm = h_ref.shape[0]
    i = pl.program_id(0)
    last = pl.num_programs(0) - 1
    slot = lax.rem(i, 2)

    def gather(dref, s):
        for t in range(tm):
            for k in range(2):
                _row_copy(bufy_ref, y_s.at[s, k], sem.at[s], dref[k * tm + t], t).start(priority=k)

    def drain(s):
        for _ in range(2 * tm):
            _row_copy(bufy_ref, y_s.at[s, 0], sem.at[s], 0, 0).wait()

    @pl.when(i == 0)
    def _():
        gather(dest_ref, 0)

    drain(slot)
    gather(dest_next_ref, 1 - slot)

    r = route_ref[...]
    h = h_ref[...] + (y_s[slot, 0] * r[:, R_G0:R_G0 + 1] + y_s[slot, 1] * r[:, R_G1:R_G1 + 1])
    gate = jax.nn.sigmoid(_dot(_rms(h, pleg_ref[...]).astype(BF16), wg_ref[...]))
    h = h + gate * _dot(p_ref[...].astype(BF16), wp_ref[...])
    out_ref[...] = _rms(h, fing_ref[...]) if final else h

    @pl.when(i == last)
    def _():
        drain(1 - slot)


def _moe_combine_ple(h, route, bufy, dest, p_all, layer, ple_g, w_gate, w_proj, final_g, final):
    t = h.shape[0]
    tm = TM_TOK
    nt = t // tm
    row_spec = pl.BlockSpec((tm, D_MODEL), lambda i: (i, 0))
    return pl.pallas_call(
        functools.partial(_ple_kernel, final=final),
        grid=(nt,),
        in_specs=[pl.BlockSpec((2 * tm,), lambda i: (i,), memory_space=pltpu.SMEM),
                  pl.BlockSpec((2 * tm,), lambda i: (jnp.minimum(i + 1, nt - 1),), memory_space=pltpu.SMEM),
                  row_spec,
                  pl.BlockSpec((tm, LANES), lambda i: (i, 0)),
                  pl.BlockSpec((None, tm, PLE_DIM), lambda i: (layer, i, 0)),
                  pl.BlockSpec(memory_space=pl.ANY),
                  _const_spec((1, D_MODEL)),
                  _const_spec((D_MODEL, D_MODEL)),
                  _const_spec((PLE_DIM, D_MODEL)),
                  _const_spec((1, D_MODEL))],
        out_specs=row_spec,
        out_shape=jax.ShapeDtypeStruct((t, D_MODEL), F32),
        scratch_shapes=[pltpu.VMEM((2, 2, tm, D_MODEL), F32), pltpu.SemaphoreType.DMA((2,))],
        compiler_params=_cparams("arbitrary"),
        name="moe_combine_ple",
    )(dest, dest, h, route, p_all, bufy, ple_g.reshape(1, -1), w_gate.astype(BF16), w_proj.astype(BF16),
      final_g.reshape(1, -1))


_P_Q = 0
_P_K = _P_Q + N_HEADS * HEAD_DIM
_P_V = _P_K + KV_DIM
_P_QI = _P_V + KV_DIM
_P_KI = _P_QI + IDX_HEADS * LANES
_P_WI = _P_KI + LANES
_P_END = _P_WI + LANES


def _rope_tables(pos, inv_ref, half):
    tm = pos.shape[0]
    lane = lax.broadcasted_iota(I32, (tm, LANES), 1)
    ang = pos * inv_ref[...]
    rot = lane < 2 * half
    cos = jnp.where(rot, jnp.cos(ang), 1.0)
    sin = jnp.sin(ang)
    s_lo = jnp.where(lane < half, -sin, 0.0)
    s_hi = jnp.where(rot & (lane >= half), sin, 0.0)
    return cos, s_lo, s_hi


def _rope(x, tables, half):
    cos, s_lo, s_hi = tables
    return x * cos + pltpu.roll(x, LANES - half, 1) * s_lo + pltpu.roll(x, half, 1) * s_hi


def _dsa_in_kernel(h_ref, g_ref, pos_ref, w_ref, invq_ref, invi_ref,
                   q_ref, k_ref, v_ref, qi_ref, ki_ref, wi_ref, hn_s):
    hn_s[...] = _rms(h_ref[...], g_ref[...]).astype(BF16)
    pos = pos_ref[...].astype(F32)
    tq = _rope_tables(pos, invq_ref, ROT_DIM // 2)
    ti = _rope_tables(pos, invi_ref, IDX_ROT_DIM // 2)

    def proj(c0, width):
        return _dot(hn_s[...], w_ref[:, c0:c0 + width])

    def rope_blocks(z, out_ref, c_out, tables, half):
        for b in range(z.shape[1] // LANES):
            blk = _rope(z[:, b * LANES:(b + 1) * LANES], tables, half)
            out_ref[:, c_out + b * LANES:c_out + (b + 1) * LANES] = blk.astype(BF16)

    nw = 512
    for c in range(0, N_HEADS * HEAD_DIM, nw):
        rope_blocks(proj(_P_Q + c, nw), q_ref, c, tq, ROT_DIM // 2)
    kv = proj(_P_K, 2 * KV_DIM)
    rope_blocks(kv[:, :KV_DIM], k_ref, 0, tq, ROT_DIM // 2)
    v_ref[...] = kv[:, KV_DIM:].astype(BF16)
    for c in range(0, IDX_HEADS * LANES, nw):
        rope_blocks(proj(_P_QI + c, nw), qi_ref, c, ti, IDX_ROT_DIM // 2)
    kw = proj(_P_KI, 2 * LANES)
    rope_blocks(kw[:, :LANES], ki_ref, 0, ti, IDX_ROT_DIM // 2)
    wi_ref[...] = kw[:, LANES:] * (IDX_HEADS ** -0.5 * IDX_DIM ** -0.5)


def _inv_freq_lanes(rot_dim):
    half = rot_dim // 2
    inv = ROPE_THETA ** (-jnp.arange(0, rot_dim, 2, dtype=F32) / rot_dim)
    lanes = jnp.zeros((LANES,), F32).at[:rot_dim].set(jnp.concatenate([inv, inv]))
    del half
    return lanes.reshape(1, LANES)


def _dsa_in_weights(w_in):
    d = w_in.shape[0]
    c_q, c_k, c_v = N_HEADS * HEAD_DIM, KV_DIM, KV_DIM
    c_qi, c_ki = IDX_HEADS * IDX_DIM, IDX_DIM
    o = np.cumsum([0, c_q, c_k, c_v, c_qi, c_ki, IDX_HEADS])
    wq, wk, wv, wqi, wki, wwi = (w_in[:, o[i]:o[i + 1]] for i in range(6))
    wqi = jnp.pad(wqi.reshape(d, IDX_HEADS, IDX_DIM), ((0, 0), (0, 0), (0, LANES - IDX_DIM)))
    wki = jnp.pad(wki, ((0, 0), (0, LANES - IDX_DIM)))
    wwi = jnp.pad(wwi, ((0, 0), (0, LANES - IDX_HEADS)))
    return jnp.concatenate([wq, wk, wv, wqi.reshape(d, IDX_HEADS * LANES), wki, wwi], axis=1).astype(BF16)


def _dsa_in(h, mix_g, positions, w_in):
    t = h.shape[0]
    tm = min(TM_MIX, t)
    spec = lambda w: pl.BlockSpec((tm, w), lambda i: (i, 0))
    widths = (N_HEADS * HEAD_DIM, KV_DIM, KV_DIM, IDX_HEADS * LANES, LANES)
    return pl.pallas_call(
        _dsa_in_kernel,
        grid=(t // tm,),
        in_specs=[spec(D_MODEL), _const_spec((1, D_MODEL)), spec(1), _const_spec((D_MODEL, _P_END)),
                  _const_spec((1, LANES)), _const_spec((1, LANES))],
        out_specs=[spec(w) for w in widths] + [spec(LANES)],
        out_shape=[jax.ShapeDtypeStruct((t, w), BF16) for w in widths]
        + [jax.ShapeDtypeStruct((t, LANES), F32)],
        scratch_shapes=[pltpu.VMEM((tm, D_MODEL), BF16)],
        compiler_params=_cparams("parallel"),
        name="dsa_in_proj",
    )(h, mix_g.reshape(1, -1), positions.reshape(t, 1), _dsa_in_weights(w_in),
      _inv_freq_lanes(ROT_DIM), _inv_freq_lanes(IDX_ROT_DIM))


_INT_MIN = -2 ** 31
_U_NEG_INF = 0x007FFFFF


def _key_to_f32(u):
    bits = jnp.where(u < 0, u ^ _INT_MIN, ~u)
    return lax.bitcast_convert_type(bits, F32)


def _dsa_core_kernel(q_ref, qi_ref, wi_ref, k_ref, v_ref, ki_ref, o_ref,
                     score_s, sel_s, bias_s, lg_s, mx_s, l_s, acc_s, *, k_top):
    qb = q_ref.shape[0]
    sk = k_ref.shape[0]
    n_fold = K_TILE // LANES
    q_end = (pl.program_id(1) + 1) * qb
    n_kt = (q_end + (K_TILE - 1)) // K_TILE
    neg_inf = jnp.float32(-jnp.inf)

    def tile_start(kt):
        return pl.multiple_of(kt * K_TILE, K_TILE)

    fold_rows = 64

    def key_fold(x, op):
        return op(x.reshape(K_TILE // fold_rows, fold_rows, qb), axis=0)

    def kq_pos(k0):
        shape = (K_TILE, qb)
        return (k0 + lax.broadcasted_iota(I32, shape, 0),
                q_end - qb + lax.broadcasted_iota(I32, shape, 1))

    qi_all = jnp.concatenate([qi_ref[:, hd * LANES:(hd + 1) * LANES] for hd in range(IDX_HEADS)], axis=0)
    wi_t = wi_ref[...].T
    wrows = [wi_t[hd:hd + 1, :] for hd in range(IDX_HEADS)]

    def score_tile(kt, c):
        k0 = tile_start(kt)
        rel = _dot_nt(ki_ref[pl.ds(k0, K_TILE), :], qi_all)
        acc = jnp.zeros((K_TILE, qb), F32)
        for hd in range(IDX_HEADS):
            acc = acc + wrows[hd] * jnp.maximum(rel[:, hd * qb:(hd + 1) * qb], 0.0)
        kpos, qpos = kq_pos(k0)
        score_s[pl.ds(k0, K_TILE), :] = jnp.where(kpos <= qpos, acc, neg_inf)
        return c

    lax.fori_loop(0, n_kt, score_tile, 0)

    def count_ge(thr):
        def body(kt, acc):
            tile = score_s[pl.ds(tile_start(kt), K_TILE), :]
            return acc + key_fold(jnp.where(tile >= thr, 1.0, 0.0), jnp.sum)

        acc = lax.fori_loop(0, n_kt, body, jnp.zeros((fold_rows, qb), F32))
        return jnp.sum(acc, axis=0, keepdims=True)

    kf = jnp.float32(k_top)

    def bisect(p, ans):
        bit = lax.shift_left(jnp.int32(1), 31 - p)
        cand = ans | bit
        force = (cand >= 0) & (cand <= _U_NEG_INF)
        ok = (count_ge(_key_to_f32(cand)) >= kf) | force
        return jnp.where(ok, cand, ans)

    ans = lax.fori_loop(0, 32, bisect, jnp.zeros((1, qb), I32))
    thr = _key_to_f32(jnp.where(ans == _U_NEG_INF, ans + 1, ans))
    thr_up = _key_to_f32(ans + 1)

    c_hi = count_ge(thr_up)
    need0 = kf - c_hi
    rem0 = count_ge(thr) - c_hi

    def init_sel(kt, c):
        k0 = tile_start(kt)
        sel_s[pl.ds(k0, K_TILE), :] = jnp.where(score_s[pl.ds(k0, K_TILE), :] >= thr_up, 0.0, NEG_BIG)
        return c

    lax.fori_loop(0, n_kt, init_sel, 0)

    def active_of(need, rem):
        return (need > 0.0) & (rem > 0.0)

    def bin_cond(carry):
        return carry[2] > 0.0

    def bin_body(carry):
        need, rem, _ = carry
        active = active_of(need, rem)

        def bin_tile(k0):
            s = score_s[pl.ds(k0, K_TILE), :]
            return s, (s >= thr) & (sel_s[pl.ds(k0, K_TILE), :] != 0.0)

        def best_of(kt, best):
            s, in_bin = bin_tile(tile_start(kt))
            return jnp.maximum(best, key_fold(jnp.where(in_bin, s, neg_inf), jnp.max))

        best = lax.fori_loop(0, n_kt, best_of, jnp.full((fold_rows, qb), neg_inf, F32))
        m = jnp.max(best, axis=0, keepdims=True)

        def first_of(kt, first):
            k0 = tile_start(kt)
            s, in_bin = bin_tile(k0)
            kpos, _ = kq_pos(k0)
            return jnp.minimum(first, key_fold(jnp.where((s == m) & in_bin, kpos, sk), jnp.min))

        first = lax.fori_loop(0, n_kt, first_of, jnp.full((fold_rows, qb), sk, I32))
        pick = jnp.min(first, axis=0, keepdims=True)

        def mark(kt, c):
            k0 = tile_start(kt)
            kpos, _ = kq_pos(k0)
            sel_s[pl.ds(k0, K_TILE), :] = jnp.where(active & (kpos == pick), 0.0, sel_s[pl.ds(k0, K_TILE), :])
            return c

        lax.fori_loop(0, n_kt, mark, 0)
        step = jnp.where(active, 1.0, 0.0)
        need, rem = need - step, rem - step
        go = jnp.max(jnp.where(active_of(need, rem), 1.0, 0.0))
        return need, rem, go

    go0 = jnp.max(jnp.where(active_of(need0, rem0), 1.0, 0.0))
    lax.while_loop(bin_cond, bin_body, (need0, rem0, go0))

    def to_rows(kt, c):
        k0 = tile_start(kt)
        bias_s[:, pl.ds(k0, K_TILE)] = sel_s[pl.ds(k0, K_TILE), :].T
        return c

    lax.fori_loop(0, n_kt, to_rows, 0)

    scale = HEAD_DIM ** -0.5 * float(np.log2(np.e))

    def q_rows(g):
        return jnp.concatenate(
            [q_ref[:, (g * HEADS_PER_KV + r) * HEAD_DIM:(g * HEADS_PER_KV + r + 1) * HEAD_DIM]
             for r in range(HEADS_PER_KV)], axis=0)

    def fold(x, op):
        out = x[:, 0:LANES]
        for c in range(1, n_fold):
            out = op(out, x[:, c * LANES:(c + 1) * LANES])
        return out

    mx_s[...] = jnp.full(mx_s.shape, NEG_BIG, F32)
    l_s[...] = jnp.zeros(l_s.shape, F32)
    acc_s[...] = jnp.zeros(acc_s.shape, F32)

    def logit_tile(kt, c):
        k0 = tile_start(kt)
        bias = jnp.concatenate([bias_s[:, pl.ds(k0, K_TILE)]] * HEADS_PER_KV, axis=0)
        for g in range(N_KV_HEADS):
            kt_ = k_ref[pl.ds(k0, K_TILE), g * HEAD_DIM:(g + 1) * HEAD_DIM]
            logits = _dot_nt(q_rows(g), kt_) * scale + bias
            lg_s[g, :, pl.ds(k0, K_TILE)] = logits
            mx_s[g] = jnp.maximum(mx_s[g], fold(logits, jnp.maximum))
        return c

    lax.fori_loop(0, n_kt, logit_tile, 0)
    row_max = [jnp.max(mx_s[g], axis=-1, keepdims=True) for g in range(N_KV_HEADS)]

    def att_tile(kt, c):
        k0 = tile_start(kt)
        for g in range(N_KV_HEADS):
            vt_ = v_ref[pl.ds(k0, K_TILE), g * HEAD_DIM:(g + 1) * HEAD_DIM]
            pexp = jnp.exp2(lg_s[g, :, pl.ds(k0, K_TILE)] - row_max[g])
            l_s[g] = l_s[g] + fold(pexp, jnp.add)
            acc_s[g] = acc_s[g] + _dot(pexp.astype(BF16), vt_)
        return c

    lax.fori_loop(0, n_kt, att_tile, 0)
    for g in range(N_KV_HEADS):
        og = acc_s[g] / jnp.sum(l_s[g], axis=-1, keepdims=True)
        for r in range(HEADS_PER_KV):
            c0 = (g * HEADS_PER_KV + r) * HEAD_DIM
            o_ref[:, c0:c0 + HEAD_DIM] = og[r * qb:(r + 1) * qb, :].astype(BF16)


def _dsa_core(q, k, v, qi, ki, wi, batch, seq):
    qb = Q_BLOCK
    assert qb == LANES and seq % K_TILE == 0
    k_top = min(TOPK_MAX, seq // 4)
    view = lambda a: a.reshape(batch, seq, a.shape[-1])
    q3, k3, v3, qi3, ki3, wi3 = (view(a) for a in (q, k, v, qi, ki, wi))
    qspec = lambda w: pl.BlockSpec((None, qb, w), lambda b, j: (b, j, 0))
    kspec = lambda w: pl.BlockSpec((None, seq, w), lambda b, j: (b, 0, 0))
    rows = HEADS_PER_KV * qb
    out = pl.pallas_call(
        functools.partial(_dsa_core_kernel, k_top=k_top),
        grid=(batch, seq // qb),
        in_specs=[qspec(D_MODEL), qspec(IDX_HEADS * LANES), qspec(LANES),
                  kspec(KV_DIM), kspec(KV_DIM), kspec(LANES)],
        out_specs=qspec(D_MODEL),
        out_shape=jax.ShapeDtypeStruct((batch, seq, D_MODEL), BF16),
        scratch_shapes=[pltpu.VMEM((seq, qb), F32), pltpu.VMEM((seq, qb), F32),
                        pltpu.VMEM((qb, seq), F32),
                        pltpu.VMEM((N_KV_HEADS, rows, seq), F32),
                        pltpu.VMEM((N_KV_HEADS, rows, LANES), F32),
                        pltpu.VMEM((N_KV_HEADS, rows, LANES), F32),
                        pltpu.VMEM((N_KV_HEADS, rows, HEAD_DIM), F32)],
        compiler_params=_cparams("parallel", "arbitrary"),
        name="dsa_core",
    )(q3, qi3, wi3, k3, v3, ki3)
    return out.reshape(batch * seq, D_MODEL)


def _dsa_out_kernel(h_ref, a_ref, wo_ref, ffng_ref, wr_ref, br_ref, h_out_ref, xn_ref, route_ref,
                    cnt_ref):
    h = h_ref[...] + _dot(a_ref[...], wo_ref[...])
    h_out_ref[...] = h
    xn = _rms(h, ffng_ref[...])
    xn_ref[...] = xn
    route_ref[...], cnt_ref[...] = _route(xn, wr_ref, br_ref)


def _dsa_out(h, att, w_out, ffn_g, wr, br):
    t = h.shape[0]
    tm = min(TM_MIX, t)
    row_spec = pl.BlockSpec((tm, D_MODEL), lambda i: (i, 0))
    return pl.pallas_call(
        _dsa_out_kernel,
        grid=(t // tm,),
        in_specs=[row_spec, row_spec, _const_spec((D_MODEL, D_MODEL)), _const_spec((1, D_MODEL)),
                  _const_spec((D_MODEL, LANES)), _const_spec((1, LANES))],
        out_specs=[row_spec, row_spec, pl.BlockSpec((tm, LANES), lambda i: (i, 0)),
                   pl.BlockSpec((8, LANES), lambda i: (i, 0))],
        out_shape=[jax.ShapeDtypeStruct((t, D_MODEL), F32), jax.ShapeDtypeStruct((t, D_MODEL), F32),
                   jax.ShapeDtypeStruct((t, LANES), F32), jax.ShapeDtypeStruct((t // tm * 8, LANES), F32)],
        compiler_params=_cparams("parallel"),
        name="dsa_out_proj",
    )(h, att, w_out.astype(BF16), ffn_g.reshape(1, -1), wr, br)


def kernel(x, p, positions, mix_norm, a_w_in, a_v_gain, a_w_s, a_b_s, a_w_out, b_w_in, b_w_out,
           ffn_norm, moe_w_group, moe_b_group, moe_w_expert, moe_b_expert, moe_w1, moe_w3, moe_w2,
           ple_norm, ple_w_gate, ple_w_proj, final_norm):
    batch, seq, d = x.shape
    depth = mix_norm.shape[0]
    t = batch * seq
    h = x.reshape(t, d)
    for i in range(depth):
        j = i // 2
        wr, br = _router_weights(moe_w_group[i], moe_b_group[i], moe_w_expert[i], moe_b_expert[i])
        if i % 2 == 0:
            h, xn, route, tile_counts = _gmlp_layer(h, mix_norm[i], a_w_in[j], a_v_gain[j], a_w_s[j],
                                                    a_b_s[j], a_w_out[j], ffn_norm[i], wr, br)
        else:
            q, k, v, qi, ki, wi = _dsa_in(h, mix_norm[i], positions, b_w_in[j])
            att = _dsa_core(q, k, v, qi, ki, wi, batch, seq)
            h, xn, route, tile_counts = _dsa_out(h, att, b_w_out[j], ffn_norm[i], wr, br)
        bufy, dest = _moe(xn, route, tile_counts, moe_w1, moe_w3, moe_w2, i)
        h = _moe_combine_ple(h, route, bufy, dest, p.reshape(depth, t, -1), i, ple_norm[i],
                             ple_w_gate[i], ple_w_proj[i], final_norm, final=(i == depth - 1))
    return h.reshape(batch, seq, d)
```

```python
import functools

import numpy as np
import jax
import jax.numpy as jnp
from jax import lax
from jax.experimental import pallas as pl
from jax.experimental.pallas import tpu as pltpu

F32 = jnp.float32
BF16 = jnp.bfloat16
I32 = jnp.int32

D_MODEL = 1024
EPS = 1e-6
ROPE_THETA = 500000.0
GM_CHUNK = 128
GM_WIDTH = 2 * D_MODEL
GM_GROUPS = 8
GM_GROUP_DIM = GM_WIDTH // GM_GROUPS
N_HEADS = 8
HEAD_DIM = D_MODEL // N_HEADS
N_KV_HEADS = 2
HEADS_PER_KV = N_HEADS // N_KV_HEADS
KV_DIM = N_KV_HEADS * HEAD_DIM
ROT_DIM = HEAD_DIM // 4
IDX_HEADS = 8
IDX_DIM = 64
IDX_ROT_DIM = IDX_DIM // 4
TOPK_MAX = 256
N_EXPERT_GROUPS = 4
EXPERTS_PER_GROUP = 8
N_EXPERTS = N_EXPERT_GROUPS * EXPERTS_PER_GROUP
D_EXPERT = D_MODEL // 2
PLE_DIM = 256

LANES = 128
V7X_VMEM_LIMIT = 56 * 1024 * 1024

TM_MIX = 512
TM_TOK = 256
TM_RANK = 512
MOE_ROWS = 256
Q_BLOCK = 128
K_TILE = 512
NEG_BIG = -1e30

R_E0, R_E1, R_G0, R_G1 = 0, 1, 2, 3
R_EXP0 = N_EXPERT_GROUPS


def _cparams(*sem):
    return pltpu.CompilerParams(dimension_semantics=sem, vmem_limit_bytes=V7X_VMEM_LIMIT)


def _const_spec(shape):
    nd = len(shape)
    return pl.BlockSpec(shape, lambda *_: (0,) * nd, pipeline_mode=pl.Buffered(1))


def _rms(x, g):
    return x * lax.rsqrt(jnp.mean(x * x, axis=-1, keepdims=True) + EPS) * g


def _gelu(x):
    return jax.nn.gelu(x)


def _dot(a, b):
    return jnp.dot(a, b, preferred_element_type=F32)


def _dot_nt(a, b):
    return lax.dot_general(a, b, (((1,), (1,)), ((), ())), preferred_element_type=F32)


def _split_bf16(a):
    hi = a.astype(BF16)
    lo = (a - hi.astype(F32)).astype(BF16)
    return hi, lo


def _dot_3pass(a, b):
    ah, al = _split_bf16(a)
    bh, bl = _split_bf16(b)
    return _dot(ah, bh) + (_dot(al, bh) + _dot(ah, bl))


def _route(xn, wr_ref, br_ref):
    tm = xn.shape[0]
    logits = _dot_3pass(xn, wr_ref[...]) + br_ref[...]
    lane = lax.broadcasted_iota(I32, (tm, LANES), 1)
    neg_inf = jnp.float32(-jnp.inf)

    gmask = lane < N_EXPERT_GROUPS
    gl = jnp.where(gmask, logits, neg_inf)
    gmax = jnp.max(gl, axis=-1, keepdims=True)
    g_sel = jnp.min(jnp.where(gl == gmax, lane, LANES), axis=-1, keepdims=True)
    p_group = 1.0 / jnp.sum(jnp.where(gmask, jnp.exp(gl - gmax), 0.0), axis=-1, keepdims=True)

    lo = R_EXP0 + EXPERTS_PER_GROUP * g_sel
    lmask = (lane >= lo) & (lane < lo + EXPERTS_PER_GROUP)
    ll = jnp.where(lmask, logits, neg_inf)
    lmax = jnp.max(ll, axis=-1, keepdims=True)
    le = jnp.where(lmask, jnp.exp(ll - lmax), 0.0)
    probs = jnp.where(lmask, le / jnp.sum(le, axis=-1, keepdims=True), -1.0)
    p1 = jnp.max(probs, axis=-1, keepdims=True)
    i1 = jnp.min(jnp.where(probs == p1, lane, LANES), axis=-1, keepdims=True)
    probs2 = jnp.where(lane == i1, -1.0, probs)
    p2 = jnp.max(probs2, axis=-1, keepdims=True)
    i2 = jnp.min(jnp.where(probs2 == p2, lane, LANES), axis=-1, keepdims=True)
    denom = p1 + p2
    g1 = p_group * p1 / denom
    g2 = p_group * p2 / denom
    e1 = (i1 - R_EXP0).astype(F32)
    e2 = (i2 - R_EXP0).astype(F32)
    out = jnp.where(lane == R_E0, e1, 0.0)
    out = jnp.where(lane == R_E1, e2, out)
    out = jnp.where(lane == R_G0, g1, out)
    out = jnp.where(lane == R_G1, g2, out)
    hits = jnp.where((lane == i1 - R_EXP0) | (lane == i2 - R_EXP0), 1.0, 0.0)
    counts = jnp.where(lax.broadcasted_iota(I32, (8, LANES), 0) == 0,
                       jnp.sum(hits, axis=0, keepdims=True), 0.0)
    return out, counts


def _router_weights(w_group, b_group, w_expert, b_expert):
    wr = jnp.zeros((D_MODEL, LANES), F32)
    wr = wr.at[:, :N_EXPERT_GROUPS].set(w_group).at[:, R_EXP0:R_EXP0 + N_EXPERTS].set(w_expert)
    br = jnp.zeros((1, LANES), F32)
    br = br.at[0, :N_EXPERT_GROUPS].set(b_group).at[0, R_EXP0:R_EXP0 + N_EXPERTS].set(b_expert)
    return wr, br


def _gmlp_kernel(x_ref, mixg_ref, win_ref, vgain_ref, ws_ref, bst_ref, wout_ref, ffng_ref, wr_ref,
                 br_ref, h_ref, xn_ref, route_ref, cnt_ref, xn_s, v_s, prod_s):
    tm = x_ref.shape[0]
    xn_s[...] = _rms(x_ref[...], mixg_ref[...]).astype(BF16)

    nt = 512
    ssq = jnp.zeros((tm, 1), F32)
    for j in range(GM_WIDTH // nt):
        z = _gelu(_dot(xn_s[...], win_ref[:, GM_WIDTH + j * nt:GM_WIDTH + (j + 1) * nt]))
        v_s[:, j * nt:(j + 1) * nt] = z
        ssq = ssq + jnp.sum(z * z, axis=-1, keepdims=True)
    rinv = lax.rsqrt(ssq * (1.0 / GM_WIDTH) + EPS)

    row = lax.broadcasted_iota(I32, (GM_CHUNK, GM_CHUNK), 0)
    col = lax.broadcasted_iota(I32, (GM_CHUNK, GM_CHUNK), 1)
    tril = row >= col
    gd = GM_GROUP_DIM
    for g in range(GM_GROUPS):
        wm = jnp.where(tril, ws_ref[g], 0.0).astype(BF16)
        vg = (v_s[:, g * gd:(g + 1) * gd] * rinv * vgain_ref[:, g * gd:(g + 1) * gd]).astype(BF16)
        u = _gelu(_dot(xn_s[...], win_ref[:, g * gd:(g + 1) * gd]))
        bias = bst_ref[:, g:g + 1]
        for c in range(tm // GM_CHUNK):
            rows = slice(c * GM_CHUNK, (c + 1) * GM_CHUNK)
            sv = _dot(wm, vg[rows, :]) + bias
            prod_s[rows, g * gd:(g + 1) * gd] = (u[rows, :] * sv).astype(BF16)

    h = x_ref[...] + _dot(prod_s[...], wout_ref[...])
    h_ref[...] = h
    xn = _rms(h, ffng_ref[...])
    xn_ref[...] = xn
    route_ref[...], cnt_ref[...] = _route(xn, wr_ref, br_ref)


def _gmlp_layer(x, mix_g, w_in, v_gain, w_s, b_s, w_out, ffn_g, wr, br):
    t = x.shape[0]
    tm = TM_MIX
    row_spec = pl.BlockSpec((tm, D_MODEL), lambda i: (i, 0))
    return pl.pallas_call(
        _gmlp_kernel,
        grid=(t // tm,),
        in_specs=[
            row_spec,
            _const_spec((1, D_MODEL)),
            _const_spec((D_MODEL, 2 * GM_WIDTH)),
            _const_spec((1, GM_WIDTH)),
            _const_spec((GM_GROUPS, GM_CHUNK, GM_CHUNK)),
            _const_spec((GM_CHUNK, GM_GROUPS)),
            _const_spec((GM_WIDTH, D_MODEL)),
            _const_spec((1, D_MODEL)),
            _const_spec((D_MODEL, LANES)),
            _const_spec((1, LANES)),
        ],
        out_specs=[row_spec, row_spec, pl.BlockSpec((tm, LANES), lambda i: (i, 0)),
                   pl.BlockSpec((8, LANES), lambda i: (i, 0))],
        out_shape=[jax.ShapeDtypeStruct((t, D_MODEL), F32), jax.ShapeDtypeStruct((t, D_MODEL), F32),
                   jax.ShapeDtypeStruct((t, LANES), F32), jax.ShapeDtypeStruct((t // tm * 8, LANES), F32)],
        scratch_shapes=[pltpu.VMEM((tm, D_MODEL), BF16), pltpu.VMEM((tm, GM_WIDTH), F32),
                        pltpu.VMEM((tm, GM_WIDTH), BF16)],
        compiler_params=_cparams("parallel"),
        name="gmlp_mixer",
    )(x, mix_g.reshape(1, -1), w_in.astype(BF16), v_gain.reshape(1, -1), w_s, b_s.T,
      w_out.astype(BF16), ffn_g.reshape(1, -1), wr, br)


def _rank_kernel(route_ref, start_ref, dest_ref, carry_s):
    tm = route_ref.shape[0]

    @pl.when(pl.program_id(0) == 0)
    def _():
        carry_s[...] = jnp.zeros_like(carry_s)

    r = route_ref[...]
    lane = lax.broadcasted_iota(I32, (tm, LANES), 1)
    lanef = lane.astype(F32)
    oh0 = (lanef == r[:, R_E0:R_E0 + 1]).astype(F32)
    oh1 = (lanef == r[:, R_E1:R_E1 + 1]).astype(F32)
    both = oh0 + oh1
    row = lax.broadcasted_iota(I32, (tm, tm), 0)
    col = lax.broadcasted_iota(I32, (tm, tm), 1)
    strict_lower = (row > col).astype(BF16)
    base = start_ref[...] + carry_s[0:1, :] + _dot(strict_lower, both.astype(BF16))
    d0 = jnp.sum(oh0 * base, axis=-1, keepdims=True)
    d1 = jnp.sum(oh1 * (base + oh0), axis=-1, keepdims=True)
    cols = jnp.where(lane == 0, d0, jnp.where(lane == 1, d1, 0.0))
    dest_ref[...] = cols.T[0:8, :].astype(I32)
    carry_s[0:1, :] = carry_s[0:1, :] + jnp.sum(both, axis=0, keepdims=True)


def _moe_rank(route, start):
    t = route.shape[0]
    tm = min(TM_RANK, t)
    nt = t // tm
    dest = pl.pallas_call(
        _rank_kernel,
        grid=(nt,),
        in_specs=[pl.BlockSpec((tm, LANES), lambda i: (i, 0)), _const_spec((1, LANES))],
        out_specs=pl.BlockSpec((8, tm), lambda i: (i, 0)),
        out_shape=jax.ShapeDtypeStruct((nt * 8, tm), I32),
        scratch_shapes=[pltpu.VMEM((8, LANES), F32)],
        compiler_params=_cparams("arbitrary"),
        name="moe_rank",
    )(route, start)
    return dest.reshape(nt, 8, tm)[:, :2, :]


def _row_copy(src_ref, dst_ref, sem, s, d):
    return pltpu.make_async_copy(src_ref.at[pl.ds(s, 1)], dst_ref.at[pl.ds(d, 1)], sem)


_DISPATCH_BUFS = 3


def _dispatch_kernel(dest_ref, xn_ref, bufx_ref, x_s, load_sem, row_sem):
    tm = TM_TOK
    i = pl.program_id(0)
    n = pl.num_programs(0)
    slot = lax.rem(i, _DISPATCH_BUFS)
    nxt = lax.rem(i + 1, _DISPATCH_BUFS)

    def load(tile, s):
        return pltpu.make_async_copy(xn_ref.at[pl.ds(tile * tm, tm)], x_s.at[s], load_sem.at[s])

    def drain_rows(s):
        for _ in range(2 * tm):
            _row_copy(x_s.at[s], bufx_ref, row_sem.at[s], 0, 0).wait()

    @pl.when(i == 0)
    def _():
        load(0, 0).start()

    @pl.when(i >= 2)
    def _():
        drain_rows(nxt)

    @pl.when(i + 1 < n)
    def _():
        load(i + 1, nxt).start()

    load(i, slot).wait()
    for t in range(tm):
        for k in range(2):
            _row_copy(x_s.at[slot], bufx_ref, row_sem.at[slot], t, dest_ref[k * tm + t]).start(priority=k)

    @pl.when(i == n - 1)
    def _():
        drain_rows(slot)

    @pl.when((i == n - 1) & (i >= 1))
    def _():
        drain_rows(lax.rem(i + 2, _DISPATCH_BUFS))


def _moe_dispatch(xn, dest_flat):
    t = xn.shape[0]
    tm = TM_TOK
    return pl.pallas_call(
        _dispatch_kernel,
        grid=(t // tm,),
        in_specs=[pl.BlockSpec((2 * tm,), lambda i: (i,), memory_space=pltpu.SMEM),
                  pl.BlockSpec(memory_space=pl.ANY)],
        out_specs=pl.BlockSpec(memory_space=pl.ANY),
        out_shape=jax.ShapeDtypeStruct((2 * t, D_MODEL), F32),
        scratch_shapes=[pltpu.VMEM((_DISPATCH_BUFS, tm, D_MODEL), F32),
                        pltpu.SemaphoreType.DMA((_DISPATCH_BUFS,)),
                        pltpu.SemaphoreType.DMA((_DISPATCH_BUFS,))],
        compiler_params=_cparams("arbitrary"),
        name="moe_dispatch",
    )(dest_flat, xn)


ROW_SLABS = D_MODEL // LANES


def _expert_kernel(ie_ref, ib_ref, ilo_ref, ihi_ref, ifirst_ref, inext_ref, islot_ref, n_ref,
                   x_ref, w1_ref, w3_ref, w2_ref, y_ref, wf1_s, wf3_s, wf2_s, w1_s, w3_s, w2_s, wsem,
                   *, layer):
    w = pl.program_id(0)
    br = x_ref.shape[0]
    live = w < n_ref[0]
    e = ie_ref[w]
    prev = ie_ref[jnp.maximum(w - 1, 0)]
    slot = islot_ref[w]

    def weight_copies(expert, s):
        return [pltpu.make_async_copy(src.at[layer, expert], dst.at[s], wsem.at[s, i])
                for i, (src, dst) in enumerate(((w1_ref, wf1_s), (w3_ref, wf3_s), (w2_ref, wf2_s)))]

    @pl.when(live & (w == 0))
    def _():
        for c in weight_copies(e, slot):
            c.start()

    @pl.when(live & ((w == 0) | (e != prev)))
    def _():
        for c in weight_copies(e, slot):
            c.wait()

        @pl.when(inext_ref[w] >= 0)
        def _():
            for c in weight_copies(inext_ref[w], 1 - slot):
                c.start()

        w1_s[...] = wf1_s[slot].astype(BF16)
        w3_s[...] = wf3_s[slot].astype(BF16)
        w2_s[...] = wf2_s[slot].astype(BF16)

    def ffn():
        rows = lax.broadcasted_iota(I32, (br, 1), 0)
        mine = (rows >= ilo_ref[w]) & (rows < ihi_ref[w])
        x = jnp.where(mine, x_ref[...], 0.0).astype(BF16)
        a = _dot(x, w1_s[...])
        b = _dot(x, w3_s[...])
        mid = (a * jax.nn.sigmoid(a) * b).astype(BF16)
        return _dot(mid, w2_s[...])

    @pl.when(live & (ifirst_ref[w] == 1))
    def _():
        y = ffn()
        for j in range(ROW_SLABS):
            y_ref[pl.ds(j, br, stride=ROW_SLABS), :] = y[:, j * LANES:(j + 1) * LANES]

    @pl.when(live & (ifirst_ref[w] == 0))
    def _():
        y = ffn()
        for j in range(ROW_SLABS):
            rows_j = pl.ds(j, br, stride=ROW_SLABS)
            y_ref[rows_j, :] = y_ref[rows_j, :] + y[:, j * LANES:(j + 1) * LANES]


def _moe_experts(bufx, items, w1, w3, w2, layer):
    n_rows = bufx.shape[0]
    br = MOE_ROWS
    n_items_max = items[0].shape[0]
    n_prefetch = len(items)

    def row_map(w, *pref):
        return (pref[1][w], 0)

    grid_spec = pltpu.PrefetchScalarGridSpec(
        num_scalar_prefetch=n_prefetch,
        grid=(n_items_max,),
        in_specs=[pl.BlockSpec((br, D_MODEL), row_map),
                  pl.BlockSpec(memory_space=pl.ANY),
                  pl.BlockSpec(memory_space=pl.ANY),
                  pl.BlockSpec(memory_space=pl.ANY)],
        out_specs=pl.BlockSpec((br * ROW_SLABS, LANES), row_map),
        scratch_shapes=[pltpu.VMEM((2, D_MODEL, D_EXPERT), F32), pltpu.VMEM((2, D_MODEL, D_EXPERT), F32),
                        pltpu.VMEM((2, D_EXPERT, D_MODEL), F32),
                        pltpu.VMEM((D_MODEL, D_EXPERT), BF16), pltpu.VMEM((D_MODEL, D_EXPERT), BF16),
                        pltpu.VMEM((D_EXPERT, D_MODEL), BF16),
                        pltpu.SemaphoreType.DMA((2, 3))],
    )
    return pl.pallas_call(
        functools.partial(_expert_kernel, layer=layer),
        grid_spec=grid_spec,
        out_shape=jax.ShapeDtypeStruct((n_rows * ROW_SLABS, LANES), F32),
        compiler_params=_cparams("arbitrary"),
        name="moe_experts",
    )(*items, bufx, w1, w3, w2)


def _moe_plan(counts, t):
    br = MOE_ROWS
    nb = 2 * t // br
    n_items_max = nb + N_EXPERTS - 1
    cnt = counts[:N_EXPERTS].astype(I32)
    end = jnp.cumsum(cnt)
    start = end - cnt

    first_blk = start // br
    n_blk = jnp.where(cnt > 0, (end - 1) // br - first_blk + 1, 0)
    iend = jnp.cumsum(n_blk)
    istart = iend - n_blk
    n_items = iend[-1:]
    w = jnp.minimum(jnp.arange(n_items_max, dtype=I32), n_items[0] - 1)
    ie = jnp.minimum(jnp.sum((iend[None, :] <= w[:, None]).astype(I32), axis=1), N_EXPERTS - 1)
    ib = first_blk[ie] + (w - istart[ie])
    ilo = jnp.maximum(start[ie], ib * br) - ib * br
    ihi = jnp.minimum(end[ie], (ib + 1) * br) - ib * br
    ifirst = jnp.concatenate([jnp.ones((1,), I32), (ib[1:] != ib[:-1]).astype(I32)])
    ids = jnp.arange(N_EXPERTS, dtype=I32)
    used = cnt > 0
    later = (ids[None, :] > ids[:, None]) & used[None, :]
    nxt = jnp.min(jnp.where(later, ids[None, :], N_EXPERTS), axis=1)
    nxt = jnp.where(nxt == N_EXPERTS, -1, nxt)
    slot = (jnp.cumsum(used.astype(I32)) - used.astype(I32)) % 2
    items = tuple(a.astype(I32) for a in (ie, ib, ilo, ihi, ifirst, nxt[ie], slot[ie], n_items))
    return items


def _moe(xn, route, tile_counts, w1, w3, w2, layer):
    t = xn.shape[0]
    counts = jnp.sum(tile_counts.reshape(-1, 8, LANES)[:, 0, :], axis=0)
    start = (jnp.cumsum(counts) - counts).reshape(1, LANES)
    dest = _moe_rank(route, start)
    nt_r, _, tm_r = dest.shape
    dest = dest.reshape(nt_r, 2, tm_r // TM_TOK, TM_TOK).transpose(0, 2, 1, 3).reshape(-1)
    items = _moe_plan(counts, t)
    bufx = _moe_dispatch(xn, dest)
    bufy = _moe_experts(bufx, items, w1, w3, w2, layer)
    return bufy, dest


def _ple_kernel(dest_ref, dest_next_ref, h_ref, route_ref, p_ref, bufy_ref, pleg_ref, wg_ref, wp_ref,
                fing_ref, out_ref, y_s, sem, *, final):
    tm = h_ref.shape[0]
    i = pl.program_id(0)
    last = pl.num_programs(0) - 1
    slot = lax.rem(i, 2)
    rs = ROW_SLABS

    def y_base(s, k):
        base = (s * 2 + k) * (tm * rs)
        return base if isinstance(base, int) else pl.multiple_of(base, tm * rs)

    def row_tile_copy(src_row, s, k, t):
        src = bufy_ref.at[pl.ds(pl.multiple_of(src_row * rs, rs), rs)]
        return pltpu.make_async_copy(src, y_s.at[pl.ds(y_base(s, k) + t * rs, rs)], sem.at[s])

    def gather(dref, s):
        for t in range(tm):
            for k in range(2):
                row_tile_copy(dref[k * tm + t], s, k, t).start(priority=k)

    def drain(s):
        for _ in range(2 * tm):
            row_tile_copy(0, s, 0, 0).wait()

    @pl.when(i == 0)
    def _():
        gather(dest_ref, 0)

    drain(slot)
    gather(dest_next_ref, 1 - slot)

    r = route_ref[...]
    g0, g1 = r[:, R_G0:R_G0 + 1], r[:, R_G1:R_G1 + 1]

    def slab(k, j):
        return y_s[pl.ds(y_base(slot, k) + j, tm, stride=rs), :]

    moe = jnp.concatenate([slab(0, j) * g0 + slab(1, j) * g1 for j in range(rs)], axis=1)
    h = h_ref[...] + moe
    gate = jax.nn.sigmoid(_dot(_rms(h, pleg_ref[...]).astype(BF16), wg_ref[...]))
    h = h + gate * _dot(p_ref[...].astype(BF16), wp_ref[...])
    out_ref[...] = _rms(h, fing_ref[...]) if final else h

    @pl.when(i == last)
    def _():
        drain(1 - slot)


def _moe_combine_ple(h, route, bufy, dest, p_all, layer, ple_g, w_gate, w_proj, final_g, final):
    t = h.shape[0]
    tm = TM_TOK
    nt = t // tm
    row_spec = pl.BlockSpec((tm, D_MODEL), lambda i: (i, 0))
    return pl.pallas_call(
        functools.partial(_ple_kernel, final=final),
        grid=(nt,),
        in_specs=[pl.BlockSpec((2 * tm,), lambda i: (i,), memory_space=pltpu.SMEM),
                  pl.BlockSpec((2 * tm,), lambda i: (jnp.minimum(i + 1, nt - 1),), memory_space=pltpu.SMEM),
                  row_spec,
                  pl.BlockSpec((tm, LANES), lambda i: (i, 0)),
                  pl.BlockSpec((None, tm, PLE_DIM), lambda i: (layer, i, 0)),
                  pl.BlockSpec(memory_space=pl.ANY),
                  _const_spec((1, D_MODEL)),
                  _const_spec((D_MODEL, D_MODEL)),
                  _const_spec((PLE_DIM, D_MODEL)),
                  _const_spec((1, D_MODEL))],
        out_specs=row_spec,
        out_shape=jax.ShapeDtypeStruct((t, D_MODEL), F32),
        scratch_shapes=[pltpu.VMEM((2 * 2 * tm * ROW_SLABS, LANES), F32), pltpu.SemaphoreType.DMA((2,))],
        compiler_params=_cparams("arbitrary"),
        name="moe_combine_ple",
    )(dest, dest, h, route, p_all, bufy, ple_g.reshape(1, -1), w_gate.astype(BF16), w_proj.astype(BF16),
      final_g.reshape(1, -1))


_P_Q = 0
_P_K = _P_Q + N_HEADS * HEAD_DIM
_P_V = _P_K + KV_DIM
_P_QI = _P_V + KV_DIM
_P_KI = _P_QI + IDX_HEADS * LANES
_P_WI = _P_KI + LANES
_P_END = _P_WI + LANES


def _rope_tables(pos, inv_ref, half):
    tm = pos.shape[0]
    lane = lax.broadcasted_iota(I32, (tm, LANES), 1)
    ang = pos * inv_ref[...]
    rot = lane < 2 * half
    cos = jnp.where(rot, jnp.cos(ang), 1.0)
    sin = jnp.sin(ang)
    s_lo = jnp.where(lane < half, -sin, 0.0)
    s_hi = jnp.where(rot & (lane >= half), sin, 0.0)
    return cos, s_lo, s_hi


def _rope(x, tables, half):
    cos, s_lo, s_hi = tables
    return x * cos + pltpu.roll(x, LANES - half, 1) * s_lo + pltpu.roll(x, half, 1) * s_hi


def _dsa_in_kernel(h_ref, g_ref, pos_ref, w_ref, invq_ref, invi_ref,
                   q_ref, k_ref, v_ref, qi_ref, ki_ref, wi_ref, hn_s):
    hn_s[...] = _rms(h_ref[...], g_ref[...]).astype(BF16)
    pos = pos_ref[...].astype(F32)
    tq = _rope_tables(pos, invq_ref, ROT_DIM // 2)
    ti = _rope_tables(pos, invi_ref, IDX_ROT_DIM // 2)

    def proj(c0, width):
        return _dot(hn_s[...], w_ref[:, c0:c0 + width])

    def rope_blocks(z, out_ref, c_out, tables, half):
        for b in range(z.shape[1] // LANES):
            blk = _rope(z[:, b * LANES:(b + 1) * LANES], tables, half)
            out_ref[:, c_out + b * LANES:c_out + (b + 1) * LANES] = blk.astype(BF16)

    nw = 512
    for c in range(0, N_HEADS * HEAD_DIM, nw):
        rope_blocks(proj(_P_Q + c, nw), q_ref, c, tq, ROT_DIM // 2)
    kv = proj(_P_K, 2 * KV_DIM)
    rope_blocks(kv[:, :KV_DIM], k_ref, 0, tq, ROT_DIM // 2)
    v_ref[...] = kv[:, KV_DIM:].astype(BF16)
    for c in range(0, IDX_HEADS * LANES, nw):
        rope_blocks(proj(_P_QI + c, nw), qi_ref, c, ti, IDX_ROT_DIM // 2)
    kw = proj(_P_KI, 2 * LANES)
    rope_blocks(kw[:, :LANES], ki_ref, 0, ti, IDX_ROT_DIM // 2)
    wi_ref[...] = kw[:, LANES:] * (IDX_HEADS ** -0.5 * IDX_DIM ** -0.5)


def _inv_freq_lanes(rot_dim):
    half = rot_dim // 2
    inv = ROPE_THETA ** (-jnp.arange(0, rot_dim, 2, dtype=F32) / rot_dim)
    lanes = jnp.zeros((LANES,), F32).at[:rot_dim].set(jnp.concatenate([inv, inv]))
    del half
    return lanes.reshape(1, LANES)


def _dsa_in_weights(w_in):
    d = w_in.shape[0]
    c_q, c_k, c_v = N_HEADS * HEAD_DIM, KV_DIM, KV_DIM
    c_qi, c_ki = IDX_HEADS * IDX_DIM, IDX_DIM
    o = np.cumsum([0, c_q, c_k, c_v, c_qi, c_ki, IDX_HEADS])
    wq, wk, wv, wqi, wki, wwi = (w_in[:, o[i]:o[i + 1]] for i in range(6))
    wqi = jnp.pad(wqi.reshape(d, IDX_HEADS, IDX_DIM), ((0, 0), (0, 0), (0, LANES - IDX_DIM)))
    wki = jnp.pad(wki, ((0, 0), (0, LANES - IDX_DIM)))
    wwi = jnp.pad(wwi, ((0, 0), (0, LANES - IDX_HEADS)))
    return jnp.concatenate([wq, wk, wv, wqi.reshape(d, IDX_HEADS * LANES), wki, wwi], axis=1).astype(BF16)


def _dsa_in(h, mix_g, positions, w_in):
    t = h.shape[0]
    tm = min(TM_MIX, t)
    spec = lambda w: pl.BlockSpec((tm, w), lambda i: (i, 0))
    widths = (N_HEADS * HEAD_DIM, KV_DIM, KV_DIM, IDX_HEADS * LANES, LANES)
    return pl.pallas_call(
        _dsa_in_kernel,
        grid=(t // tm,),
        in_specs=[spec(D_MODEL), _const_spec((1, D_MODEL)), spec(1), _const_spec((D_MODEL, _P_END)),
                  _const_spec((1, LANES)), _const_spec((1, LANES))],
        out_specs=[spec(w) for w in widths] + [spec(LANES)],
        out_shape=[jax.ShapeDtypeStruct((t, w), BF16) for w in widths]
        + [jax.ShapeDtypeStruct((t, LANES), F32)],
        scratch_shapes=[pltpu.VMEM((tm, D_MODEL), BF16)],
        compiler_params=_cparams("parallel"),
        name="dsa_in_proj",
    )(h, mix_g.reshape(1, -1), positions.reshape(t, 1), _dsa_in_weights(w_in),
      _inv_freq_lanes(ROT_DIM), _inv_freq_lanes(IDX_ROT_DIM))


_INT_MIN = -2 ** 31
_U_NEG_INF = 0x007FFFFF


def _key_to_f32(u):
    bits = jnp.where(u < 0, u ^ _INT_MIN, ~u)
    return lax.bitcast_convert_type(bits, F32)


def _dsa_core_kernel(q_ref, qi_ref, wi_ref, k_ref, v_ref, ki_ref, o_ref,
                     score_s, sel_s, bias_s, lg_s, mx_s, l_s, acc_s, *, k_top):
    qb = q_ref.shape[0]
    sk = k_ref.shape[0]
    n_fold = K_TILE // LANES
    q_end = (pl.program_id(1) + 1) * qb
    n_kt = (q_end + (K_TILE - 1)) // K_TILE
    neg_inf = jnp.float32(-jnp.inf)

    def tile_start(kt):
        return pl.multiple_of(kt * K_TILE, K_TILE)

    fold_rows = 64

    def key_fold(x, op):
        return op(x.reshape(K_TILE // fold_rows, fold_rows, qb), axis=0)

    def kq_pos(k0):
        shape = (K_TILE, qb)
        return (k0 + lax.broadcasted_iota(I32, shape, 0),
                q_end - qb + lax.broadcasted_iota(I32, shape, 1))

    qi_all = jnp.concatenate([qi_ref[:, hd * LANES:(hd + 1) * LANES] for hd in range(IDX_HEADS)], axis=0)
    wi_t = wi_ref[...].T
    wrows = [wi_t[hd:hd + 1, :] for hd in range(IDX_HEADS)]

    def score_tile(kt, c):
        k0 = tile_start(kt)
        rel = _dot_nt(ki_ref[pl.ds(k0, K_TILE), :], qi_all)
        acc = jnp.zeros((K_TILE, qb), F32)
        for hd in range(IDX_HEADS):
            acc = acc + wrows[hd] * jnp.maximum(rel[:, hd * qb:(hd + 1) * qb], 0.0)
        kpos, qpos = kq_pos(k0)
        score_s[pl.ds(k0, K_TILE), :] = jnp.where(kpos <= qpos, acc, neg_inf)
        return c

    lax.fori_loop(0, n_kt, score_tile, 0)

    def count_ge(thr):
        def body(kt, acc):
            tile = score_s[pl.ds(tile_start(kt), K_TILE), :]
            return acc + key_fold(jnp.where(tile >= thr, 1.0, 0.0), jnp.sum)

        acc = lax.fori_loop(0, n_kt, body, jnp.zeros((fold_rows, qb), F32))
        return jnp.sum(acc, axis=0, keepdims=True)

    kf = jnp.float32(k_top)

    def bisect(p, ans):
        bit = lax.shift_left(jnp.int32(1), 31 - p)
        cand = ans | bit
        force = (cand >= 0) & (cand <= _U_NEG_INF)
        ok = (count_ge(_key_to_f32(cand)) >= kf) | force
        return jnp.where(ok, cand, ans)

    ans = lax.fori_loop(0, 32, bisect, jnp.zeros((1, qb), I32))
    thr = _key_to_f32(jnp.where(ans == _U_NEG_INF, ans + 1, ans))
    thr_up = _key_to_f32(ans + 1)

    c_hi = count_ge(thr_up)
    need0 = kf - c_hi
    rem0 = count_ge(thr) - c_hi

    def init_sel(kt, c):
        k0 = tile_start(kt)
        sel_s[pl.ds(k0, K_TILE), :] = jnp.where(score_s[pl.ds(k0, K_TILE), :] >= thr_up, 0.0, NEG_BIG)
        return c

    lax.fori_loop(0, n_kt, init_sel, 0)

    def active_of(need, rem):
        return (need > 0.0) & (rem > 0.0)

    def bin_cond(carry):
        return carry[2] > 0.0

    def bin_body(carry):
        need, rem, _ = carry
        active = active_of(need, rem)

        def bin_tile(k0):
            s = score_s[pl.ds(k0, K_TILE), :]
            return s, (s >= thr) & (sel_s[pl.ds(k0, K_TILE), :] != 0.0)

        def best_of(kt, best):
            s, in_bin = bin_tile(tile_start(kt))
            return jnp.maximum(best, key_fold(jnp.where(in_bin, s, neg_inf), jnp.max))

        best = lax.fori_loop(0, n_kt, best_of, jnp.full((fold_rows, qb), neg_inf, F32))
        m = jnp.max(best, axis=0, keepdims=True)

        def first_of(kt, first):
            k0 = tile_start(kt)
            s, in_bin = bin_tile(k0)
            kpos, _ = kq_pos(k0)
            return jnp.minimum(first, key_fold(jnp.where((s == m) & in_bin, kpos, sk), jnp.min))

        first = lax.fori_loop(0, n_kt, first_of, jnp.full((fold_rows, qb), sk, I32))
        pick = jnp.min(first, axis=0, keepdims=True)

        def mark(kt, c):
            k0 = tile_start(kt)
            kpos, _ = kq_pos(k0)
            sel_s[pl.ds(k0, K_TILE), :] = jnp.where(active & (kpos == pick), 0.0, sel_s[pl.ds(k0, K_TILE), :])
            return c

        lax.fori_loop(0, n_kt, mark, 0)
        step = jnp.where(active, 1.0, 0.0)
        need, rem = need - step, rem - step
        go = jnp.max(jnp.where(active_of(need, rem), 1.0, 0.0))
        return need, rem, go

    go0 = jnp.max(jnp.where(active_of(need0, rem0), 1.0, 0.0))
    lax.while_loop(bin_cond, bin_body, (need0, rem0, go0))

    def to_rows(kt, c):
        k0 = tile_start(kt)
        bias_s[:, pl.ds(k0, K_TILE)] = sel_s[pl.ds(k0, K_TILE), :].T
        return c

    lax.fori_loop(0, n_kt, to_rows, 0)

    scale = HEAD_DIM ** -0.5 * float(np.log2(np.e))

    def q_rows(g):
        return jnp.concatenate(
            [q_ref[:, (g * HEADS_PER_KV + r) * HEAD_DIM:(g * HEADS_PER_KV + r + 1) * HEAD_DIM]
             for r in range(HEADS_PER_KV)], axis=0)

    def fold(x, op):
        out = x[:, 0:LANES]
        for c in range(1, n_fold):
            out = op(out, x[:, c * LANES:(c + 1) * LANES])
        return out

    mx_s[...] = jnp.full(mx_s.shape, NEG_BIG, F32)
    l_s[...] = jnp.zeros(l_s.shape, F32)
    acc_s[...] = jnp.zeros(acc_s.shape, F32)

    def logit_tile(kt, c):
        k0 = tile_start(kt)
        bias = jnp.concatenate([bias_s[:, pl.ds(k0, K_TILE)]] * HEADS_PER_KV, axis=0)
        for g in range(N_KV_HEADS):
            kt_ = k_ref[pl.ds(k0, K_TILE), g * HEAD_DIM:(g + 1) * HEAD_DIM]
            logits = _dot_nt(q_rows(g), kt_) * scale + bias
            lg_s[g, :, pl.ds(k0, K_TILE)] = logits
            mx_s[g] = jnp.maximum(mx_s[g], fold(logits, jnp.maximum))
        return c

    lax.fori_loop(0, n_kt, logit_tile, 0)
    row_max = [jnp.max(mx_s[g], axis=-1, keepdims=True) for g in range(N_KV_HEADS)]

    def att_tile(kt, c):
        k0 = tile_start(kt)
        for g in range(N_KV_HEADS):
            vt_ = v_ref[pl.ds(k0, K_TILE), g * HEAD_DIM:(g + 1) * HEAD_DIM]
            pexp = jnp.exp2(lg_s[g, :, pl.ds(k0, K_TILE)] - row_max[g])
            l_s[g] = l_s[g] + fold(pexp, jnp.add)
            acc_s[g] = acc_s[g] + _dot(pexp.astype(BF16), vt_)
        return c

    lax.fori_loop(0, n_kt, att_tile, 0)
    for g in range(N_KV_HEADS):
        og = acc_s[g] / jnp.sum(l_s[g], axis=-1, keepdims=True)
        for r in range(HEADS_PER_KV):
            c0 = (g * HEADS_PER_KV + r) * HEAD_DIM
            o_ref[:, c0:c0 + HEAD_DIM] = og[r * qb:(r + 1) * qb, :].astype(BF16)


def _dsa_core(q, k, v, qi, ki, wi, batch, seq):
    qb = Q_BLOCK
    assert qb == LANES and seq % K_TILE == 0
    k_top = min(TOPK_MAX, seq // 4)
    view = lambda a: a.reshape(batch, seq, a.shape[-1])
    q3, k3, v3, qi3, ki3, wi3 = (view(a) for a in (q, k, v, qi, ki, wi))
    qspec = lambda w: pl.BlockSpec((None, qb, w), lambda b, j: (b, j, 0))
    kspec = lambda w: pl.BlockSpec((None, seq, w), lambda b, j: (b, 0, 0))
    rows = HEADS_PER_KV * qb
    out = pl.pallas_call(
        functools.partial(_dsa_core_kernel, k_top=k_top),
        grid=(batch, seq // qb),
        in_specs=[qspec(D_MODEL), qspec(IDX_HEADS * LANES), qspec(LANES),
                  kspec(KV_DIM), kspec(KV_DIM), kspec(LANES)],
        out_specs=qspec(D_MODEL),
        out_shape=jax.ShapeDtypeStruct((batch, seq, D_MODEL), BF16),
        scratch_shapes=[pltpu.VMEM((seq, qb), F32), pltpu.VMEM((seq, qb), F32),
                        pltpu.VMEM((qb, seq), F32),
                        pltpu.VMEM((N_KV_HEADS, rows, seq), F32),
                        pltpu.VMEM((N_KV_HEADS, rows, LANES), F32),
                        pltpu.VMEM((N_KV_HEADS, rows, LANES), F32),
                        pltpu.VMEM((N_KV_HEADS, rows, HEAD_DIM), F32)],
        compiler_params=_cparams("parallel", "arbitrary"),
        name="dsa_core",
    )(q3, qi3, wi3, k3, v3, ki3)
    return out.reshape(batch * seq, D_MODEL)


def _dsa_out_kernel(h_ref, a_ref, wo_ref, ffng_ref, wr_ref, br_ref, h_out_ref, xn_ref, route_ref,
                    cnt_ref):
    h = h_ref[...] + _dot(a_ref[...], wo_ref[...])
    h_out_ref[...] = h
    xn = _rms(h, ffng_ref[...])
    xn_ref[...] = xn
    route_ref[...], cnt_ref[...] = _route(xn, wr_ref, br_ref)


def _dsa_out(h, att, w_out, ffn_g, wr, br):
    t = h.shape[0]
    tm = min(TM_MIX, t)
    row_spec = pl.BlockSpec((tm, D_MODEL), lambda i: (i, 0))
    return pl.pallas_call(
        _dsa_out_kernel,
        grid=(t // tm,),
        in_specs=[row_spec, row_spec, _const_spec((D_MODEL, D_MODEL)), _const_spec((1, D_MODEL)),
                  _const_spec((D_MODEL, LANES)), _const_spec((1, LANES))],
        out_specs=[row_spec, row_spec, pl.BlockSpec((tm, LANES), lambda i: (i, 0)),
                   pl.BlockSpec((8, LANES), lambda i: (i, 0))],
        out_shape=[jax.ShapeDtypeStruct((t, D_MODEL), F32), jax.ShapeDtypeStruct((t, D_MODEL), F32),
                   jax.ShapeDtypeStruct((t, LANES), F32), jax.ShapeDtypeStruct((t // tm * 8, LANES), F32)],
        compiler_params=_cparams("parallel"),
        name="dsa_out_proj",
    )(h, att, w_out.astype(BF16), ffn_g.reshape(1, -1), wr, br)


def kernel(x, p, positions, mix_norm, a_w_in, a_v_gain, a_w_s, a_b_s, a_w_out, b_w_in, b_w_out,
           ffn_norm, moe_w_group, moe_b_group, moe_w_expert, moe_b_expert, moe_w1, moe_w3, moe_w2,
           ple_norm, ple_w_gate, ple_w_proj, final_norm):
    batch, seq, d = x.shape
    depth = mix_norm.shape[0]
    t = batch * seq
    h = x.reshape(t, d)
    for i in range(depth):
        j = i // 2
        wr, br = _router_weights(moe_w_group[i], moe_b_group[i], moe_w_expert[i], moe_b_expert[i])
        if i % 2 == 0:
            h, xn, route, tile_counts = _gmlp_layer(h, mix_norm[i], a_w_in[j], a_v_gain[j], a_w_s[j],
                                                    a_b_s[j], a_w_out[j], ffn_norm[i], wr, br)
        else:
            q, k, v, qi, ki, wi = _dsa_in(h, mix_norm[i], positions, b_w_in[j])
            att = _dsa_core(q, k, v, qi, ki, wi, batch, seq)
            h, xn, route, tile_counts = _dsa_out(h, att, b_w_out[j], ffn_norm[i], wr, br)
        bufy, dest = _moe(xn, route, tile_counts, moe_w1, moe_w3, moe_w2, i)
        h = _moe_combine_ple(h, route, bufy, dest, p.reshape(depth, t, -1), i, ple_norm[i],
                             ple_w_gate[i], ple_w_proj[i], final_norm, final=(i == depth - 1))
    return h.reshape(batch, seq, d)
```

```python
import functools

import numpy as np
import jax
import jax.numpy as jnp
from jax import lax
from jax.experimental import pallas as pl
from jax.experimental.pallas import tpu as pltpu

F32 = jnp.float32
BF16 = jnp.bfloat16
I32 = jnp.int32

D_MODEL = 1024
EPS = 1e-6
ROPE_THETA = 500000.0
GM_CHUNK = 128
GM_WIDTH = 2 * D_MODEL
GM_GROUPS = 8
GM_GROUP_DIM = GM_WIDTH // GM_GROUPS
N_HEADS = 8
HEAD_DIM = D_MODEL // N_HEADS
N_KV_HEADS = 2
HEADS_PER_KV = N_HEADS // N_KV_HEADS
KV_DIM = N_KV_HEADS * HEAD_DIM
ROT_DIM = HEAD_DIM // 4
IDX_HEADS = 8
IDX_DIM = 64
IDX_ROT_DIM = IDX_DIM // 4
TOPK_MAX = 256
N_EXPERT_GROUPS = 4
EXPERTS_PER_GROUP = 8
N_EXPERTS = N_EXPERT_GROUPS * EXPERTS_PER_GROUP
D_EXPERT = D_MODEL // 2
PLE_DIM = 256

LANES = 128
V7X_VMEM_LIMIT = 56 * 1024 * 1024

TM_MIX = 512
TM_TOK = 256
TM_RANK = 512
MOE_ROWS = 256
Q_BLOCK = 128
K_TILE = 512
NEG_BIG = -1e30

R_E0, R_E1, R_G0, R_G1 = 0, 1, 2, 3
R_EXP0 = N_EXPERT_GROUPS


def _cparams(*sem):
    return pltpu.CompilerParams(dimension_semantics=sem, vmem_limit_bytes=V7X_VMEM_LIMIT)


def _const_spec(shape):
    nd = len(shape)
    return pl.BlockSpec(shape, lambda *_: (0,) * nd, pipeline_mode=pl.Buffered(1))


def _rms(x, g):
    return x * lax.rsqrt(jnp.mean(x * x, axis=-1, keepdims=True) + EPS) * g


def _gelu(x):
    return jax.nn.gelu(x)


def _dot(a, b):
    return jnp.dot(a, b, preferred_element_type=F32)


def _dot_nt(a, b):
    return lax.dot_general(a, b, (((1,), (1,)), ((), ())), preferred_element_type=F32)


def _split_bf16(a):
    hi = a.astype(BF16)
    lo = (a - hi.astype(F32)).astype(BF16)
    return hi, lo


def _dot_3pass(a, b):
    ah, al = _split_bf16(a)
    bh, bl = _split_bf16(b)
    return _dot(ah, bh) + (_dot(al, bh) + _dot(ah, bl))


def _route(xn, wr_ref, br_ref):
    tm = xn.shape[0]
    logits = _dot_3pass(xn, wr_ref[...]) + br_ref[...]
    lane = lax.broadcasted_iota(I32, (tm, LANES), 1)
    neg_inf = jnp.float32(-jnp.inf)

    gmask = lane < N_EXPERT_GROUPS
    gl = jnp.where(gmask, logits, neg_inf)
    gmax = jnp.max(gl, axis=-1, keepdims=True)
    g_sel = jnp.min(jnp.where(gl == gmax, lane, LANES), axis=-1, keepdims=True)
    p_group = 1.0 / jnp.sum(jnp.where(gmask, jnp.exp(gl - gmax), 0.0), axis=-1, keepdims=True)

    lo = R_EXP0 + EXPERTS_PER_GROUP * g_sel
    lmask = (lane >= lo) & (lane < lo + EXPERTS_PER_GROUP)
    ll = jnp.where(lmask, logits, neg_inf)
    lmax = jnp.max(ll, axis=-1, keepdims=True)
    le = jnp.where(lmask, jnp.exp(ll - lmax), 0.0)
    probs = jnp.where(lmask, le / jnp.sum(le, axis=-1, keepdims=True), -1.0)
    p1 = jnp.max(probs, axis=-1, keepdims=True)
    i1 = jnp.min(jnp.where(probs == p1, lane, LANES), axis=-1, keepdims=True)
    probs2 = jnp.where(lane == i1, -1.0, probs)
    p2 = jnp.max(probs2, axis=-1, keepdims=True)
    i2 = jnp.min(jnp.where(probs2 == p2, lane, LANES), axis=-1, keepdims=True)
    denom = p1 + p2
    g1 = p_group * p1 / denom
    g2 = p_group * p2 / denom
    e1 = (i1 - R_EXP0).astype(F32)
    e2 = (i2 - R_EXP0).astype(F32)
    out = jnp.where(lane == R_E0, e1, 0.0)
    out = jnp.where(lane == R_E1, e2, out)
    out = jnp.where(lane == R_G0, g1, out)
    out = jnp.where(lane == R_G1, g2, out)
    hits = jnp.where((lane == i1 - R_EXP0) | (lane == i2 - R_EXP0), 1.0, 0.0)
    counts = jnp.where(lax.broadcasted_iota(I32, (8, LANES), 0) == 0,
                       jnp.sum(hits, axis=0, keepdims=True), 0.0)
    return out, counts


def _router_weights(w_group, b_group, w_expert, b_expert):
    wr = jnp.zeros((D_MODEL, LANES), F32)
    wr = wr.at[:, :N_EXPERT_GROUPS].set(w_group).at[:, R_EXP0:R_EXP0 + N_EXPERTS].set(w_expert)
    br = jnp.zeros((1, LANES), F32)
    br = br.at[0, :N_EXPERT_GROUPS].set(b_group).at[0, R_EXP0:R_EXP0 + N_EXPERTS].set(b_expert)
    return wr, br


def _gmlp_kernel(x_ref, mixg_ref, win_ref, vgain_ref, ws_ref, bst_ref, wout_ref, ffng_ref, wr_ref,
                 br_ref, h_ref, xn_ref, route_ref, cnt_ref, xn_s, v_s, prod_s):
    tm = x_ref.shape[0]
    xn_s[...] = _rms(x_ref[...], mixg_ref[...]).astype(BF16)

    nt = 512
    ssq = jnp.zeros((tm, 1), F32)
    for j in range(GM_WIDTH // nt):
        z = _gelu(_dot(xn_s[...], win_ref[:, GM_WIDTH + j * nt:GM_WIDTH + (j + 1) * nt]))
        v_s[:, j * nt:(j + 1) * nt] = z
        ssq = ssq + jnp.sum(z * z, axis=-1, keepdims=True)
    rinv = lax.rsqrt(ssq * (1.0 / GM_WIDTH) + EPS)

    row = lax.broadcasted_iota(I32, (GM_CHUNK, GM_CHUNK), 0)
    col = lax.broadcasted_iota(I32, (GM_CHUNK, GM_CHUNK), 1)
    tril = row >= col
    gd = GM_GROUP_DIM
    for g in range(GM_GROUPS):
        wm = jnp.where(tril, ws_ref[g], 0.0).astype(BF16)
        vg = (v_s[:, g * gd:(g + 1) * gd] * rinv * vgain_ref[:, g * gd:(g + 1) * gd]).astype(BF16)
        u = _gelu(_dot(xn_s[...], win_ref[:, g * gd:(g + 1) * gd]))
        bias = bst_ref[:, g:g + 1]
        for c in range(tm // GM_CHUNK):
            rows = slice(c * GM_CHUNK, (c + 1) * GM_CHUNK)
            sv = _dot(wm, vg[rows, :]) + bias
            prod_s[rows, g * gd:(g + 1) * gd] = (u[rows, :] * sv).astype(BF16)

    h = x_ref[...] + _dot(prod_s[...], wout_ref[...])
    h_ref[...] = h
    xn = _rms(h, ffng_ref[...])
    xn_ref[...] = xn
    route_ref[...], cnt_ref[...] = _route(xn, wr_ref, br_ref)


def _gmlp_layer(x, mix_g, w_in, v_gain, w_s, b_s, w_out, ffn_g, wr, br):
    t = x.shape[0]
    tm = TM_MIX
    row_spec = pl.BlockSpec((tm, D_MODEL), lambda i: (i, 0))
    return pl.pallas_call(
        _gmlp_kernel,
        grid=(t // tm,),
        in_specs=[
            row_spec,
            _const_spec((1, D_MODEL)),
            _const_spec((D_MODEL, 2 * GM_WIDTH)),
            _const_spec((1, GM_WIDTH)),
            _const_spec((GM_GROUPS, GM_CHUNK, GM_CHUNK)),
            _const_spec((GM_CHUNK, GM_GROUPS)),
            _const_spec((GM_WIDTH, D_MODEL)),
            _const_spec((1, D_MODEL)),
            _const_spec((D_MODEL, LANES)),
            _const_spec((1, LANES)),
        ],
        out_specs=[row_spec, row_spec, pl.BlockSpec((tm, LANES), lambda i: (i, 0)),
                   pl.BlockSpec((8, LANES), lambda i: (i, 0))],
        out_shape=[jax.ShapeDtypeStruct((t, D_MODEL), F32), jax.ShapeDtypeStruct((t, D_MODEL), F32),
                   jax.ShapeDtypeStruct((t, LANES), F32), jax.ShapeDtypeStruct((t // tm * 8, LANES), F32)],
        scratch_shapes=[pltpu.VMEM((tm, D_MODEL), BF16), pltpu.VMEM((tm, GM_WIDTH), F32),
                        pltpu.VMEM((tm, GM_WIDTH), BF16)],
        compiler_params=_cparams("parallel"),
        name="gmlp_mixer",
    )(x, mix_g.reshape(1, -1), w_in.astype(BF16), v_gain.reshape(1, -1), w_s, b_s.T,
      w_out.astype(BF16), ffn_g.reshape(1, -1), wr, br)


def _rank_kernel(route_ref, start_ref, dest_ref, carry_s):
    tm = route_ref.shape[0]

    @pl.when(pl.program_id(0) == 0)
    def _():
        carry_s[...] = jnp.zeros_like(carry_s)

    r = route_ref[...]
    lane = lax.broadcasted_iota(I32, (tm, LANES), 1)
    lanef = lane.astype(F32)
    oh0 = (lanef == r[:, R_E0:R_E0 + 1]).astype(F32)
    oh1 = (lanef == r[:, R_E1:R_E1 + 1]).astype(F32)
    both = oh0 + oh1
    row = lax.broadcasted_iota(I32, (tm, tm), 0)
    col = lax.broadcasted_iota(I32, (tm, tm), 1)
    strict_lower = (row > col).astype(BF16)
    base = start_ref[...] + carry_s[0:1, :] + _dot(strict_lower, both.astype(BF16))
    d0 = jnp.sum(oh0 * base, axis=-1, keepdims=True)
    d1 = jnp.sum(oh1 * (base + oh0), axis=-1, keepdims=True)
    cols = jnp.where(lane == 0, d0, jnp.where(lane == 1, d1, 0.0))
    dest_ref[...] = cols.T[0:8, :].astype(I32)
    carry_s[0:1, :] = carry_s[0:1, :] + jnp.sum(both, axis=0, keepdims=True)


def _moe_rank(route, start):
    t = route.shape[0]
    tm = min(TM_RANK, t)
    nt = t // tm
    dest = pl.pallas_call(
        _rank_kernel,
        grid=(nt,),
        in_specs=[pl.BlockSpec((tm, LANES), lambda i: (i, 0)), _const_spec((1, LANES))],
        out_specs=pl.BlockSpec((8, tm), lambda i: (i, 0)),
        out_shape=jax.ShapeDtypeStruct((nt * 8, tm), I32),
        scratch_shapes=[pltpu.VMEM((8, LANES), F32)],
        compiler_params=_cparams("arbitrary"),
        name="moe_rank",
    )(route, start)
    return dest.reshape(nt, 8, tm)[:, :2, :]


def _row_copy(src_ref, dst_ref, sem, s, d):
    return pltpu.make_async_copy(src_ref.at[pl.ds(s, 1)], dst_ref.at[pl.ds(d, 1)], sem)


_DISPATCH_BUFS = 3


def _dispatch_kernel(dest_ref, xn_ref, bufx_ref, x_s, load_sem, row_sem):
    tm = TM_TOK
    i = pl.program_id(0)
    n = pl.num_programs(0)
    slot = lax.rem(i, _DISPATCH_BUFS)
    nxt = lax.rem(i + 1, _DISPATCH_BUFS)

    def load(tile, s):
        return pltpu.make_async_copy(xn_ref.at[pl.ds(tile * tm, tm)], x_s.at[s], load_sem.at[s])

    def drain_rows(s):
        for _ in range(2 * tm):
            _row_copy(x_s.at[s], bufx_ref, row_sem.at[s], 0, 0).wait()

    @pl.when(i == 0)
    def _():
        load(0, 0).start()

    @pl.when(i >= 2)
    def _():
        drain_rows(nxt)

    @pl.when(i + 1 < n)
    def _():
        load(i + 1, nxt).start()

    load(i, slot).wait()
    for t in range(tm):
        for k in range(2):
            _row_copy(x_s.at[slot], bufx_ref, row_sem.at[slot], t, dest_ref[k * tm + t]).start(priority=k)

    @pl.when(i == n - 1)
    def _():
        drain_rows(slot)

    @pl.when((i == n - 1) & (i >= 1))
    def _():
        drain_rows(lax.rem(i + 2, _DISPATCH_BUFS))


def _moe_dispatch(xn, dest_flat):
    t = xn.shape[0]
    tm = TM_TOK
    return pl.pallas_call(
        _dispatch_kernel,
        grid=(t // tm,),
        in_specs=[pl.BlockSpec((2 * tm,), lambda i: (i,), memory_space=pltpu.SMEM),
                  pl.BlockSpec(memory_space=pl.ANY)],
        out_specs=pl.BlockSpec(memory_space=pl.ANY),
        out_shape=jax.ShapeDtypeStruct((2 * t, D_MODEL), F32),
        scratch_shapes=[pltpu.VMEM((_DISPATCH_BUFS, tm, D_MODEL), F32),
                        pltpu.SemaphoreType.DMA((_DISPATCH_BUFS,)),
                        pltpu.SemaphoreType.DMA((_DISPATCH_BUFS,))],
        compiler_params=_cparams("arbitrary"),
        name="moe_dispatch",
    )(dest_flat, xn)


ROW_SLABS = D_MODEL // LANES


def _expert_kernel(ie_ref, ib_ref, ilo_ref, ihi_ref, ifirst_ref, inext_ref, islot_ref, n_ref,
                   x_ref, w1_ref, w3_ref, w2_ref, y_ref, wf1_s, wf3_s, wf2_s, w1_s, w3_s, w2_s, wsem,
                   *, layer):
    w = pl.program_id(0)
    br = x_ref.shape[0]
    live = w < n_ref[0]
    e = ie_ref[w]
    prev = ie_ref[jnp.maximum(w - 1, 0)]
    slot = islot_ref[w]

    def weight_copies(expert, s):
        return [pltpu.make_async_copy(src.at[layer, expert], dst.at[s], wsem.at[s, i])
                for i, (src, dst) in enumerate(((w1_ref, wf1_s), (w3_ref, wf3_s), (w2_ref, wf2_s)))]

    @pl.when(live & (w == 0))
    def _():
        for c in weight_copies(e, slot):
            c.start()

    @pl.when(live & ((w == 0) | (e != prev)))
    def _():
        for c in weight_copies(e, slot):
            c.wait()

        @pl.when(inext_ref[w] >= 0)
        def _():
            for c in weight_copies(inext_ref[w], 1 - slot):
                c.start()

        w1_s[...] = wf1_s[slot].astype(BF16)
        w3_s[...] = wf3_s[slot].astype(BF16)
        w2_s[...] = wf2_s[slot].astype(BF16)

    def ffn():
        rows = lax.broadcasted_iota(I32, (br, 1), 0)
        mine = (rows >= ilo_ref[w]) & (rows < ihi_ref[w])
        x = jnp.where(mine, x_ref[...], 0.0).astype(BF16)
        a = _dot(x, w1_s[...])
        b = _dot(x, w3_s[...])
        mid = (a * jax.nn.sigmoid(a) * b).astype(BF16)
        return _dot(mid, w2_s[...])

    @pl.when(live & (ifirst_ref[w] == 1))
    def _():
        y = ffn()
        for j in range(ROW_SLABS):
            y_ref[pl.ds(j, br, stride=ROW_SLABS), :] = y[:, j * LANES:(j + 1) * LANES]

    @pl.when(live & (ifirst_ref[w] == 0))
    def _():
        y = ffn()
        for j in range(ROW_SLABS):
            rows_j = pl.ds(j, br, stride=ROW_SLABS)
            y_ref[rows_j, :] = y_ref[rows_j, :] + y[:, j * LANES:(j + 1) * LANES]


def _moe_experts(bufx, items, w1, w3, w2, layer):
    n_rows = bufx.shape[0]
    br = MOE_ROWS
    n_items_max = items[0].shape[0]
    n_prefetch = len(items)

    def row_map(w, *pref):
        return (pref[1][w], 0)

    grid_spec = pltpu.PrefetchScalarGridSpec(
        num_scalar_prefetch=n_prefetch,
        grid=(n_items_max,),
        in_specs=[pl.BlockSpec((br, D_MODEL), row_map),
                  pl.BlockSpec(memory_space=pl.ANY),
                  pl.BlockSpec(memory_space=pl.ANY),
                  pl.BlockSpec(memory_space=pl.ANY)],
        out_specs=pl.BlockSpec((br * ROW_SLABS, LANES), row_map),
        scratch_shapes=[pltpu.VMEM((2, D_MODEL, D_EXPERT), F32), pltpu.VMEM((2, D_MODEL, D_EXPERT), F32),
                        pltpu.VMEM((2, D_EXPERT, D_MODEL), F32),
                        pltpu.VMEM((D_MODEL, D_EXPERT), BF16), pltpu.VMEM((D_MODEL, D_EXPERT), BF16),
                        pltpu.VMEM((D_EXPERT, D_MODEL), BF16),
                        pltpu.SemaphoreType.DMA((2, 3))],
    )
    return pl.pallas_call(
        functools.partial(_expert_kernel, layer=layer),
        grid_spec=grid_spec,
        out_shape=jax.ShapeDtypeStruct((n_rows * ROW_SLABS, LANES), F32),
        compiler_params=_cparams("arbitrary"),
        name="moe_experts",
    )(*items, bufx, w1, w3, w2)


def _moe_plan(counts, t):
    br = MOE_ROWS
    nb = 2 * t // br
    n_items_max = nb + N_EXPERTS - 1
    cnt = counts[:N_EXPERTS].astype(I32)
    end = jnp.cumsum(cnt)
    start = end - cnt

    first_blk = start // br
    n_blk = jnp.where(cnt > 0, (end - 1) // br - first_blk + 1, 0)
    iend = jnp.cumsum(n_blk)
    istart = iend - n_blk
    n_items = iend[-1:]
    w = jnp.minimum(jnp.arange(n_items_max, dtype=I32), n_items[0] - 1)
    ie = jnp.minimum(jnp.sum((iend[None, :] <= w[:, None]).astype(I32), axis=1), N_EXPERTS - 1)
    ib = first_blk[ie] + (w - istart[ie])
    ilo = jnp.maximum(start[ie], ib * br) - ib * br
    ihi = jnp.minimum(end[ie], (ib + 1) * br) - ib * br
    ifirst = jnp.concatenate([jnp.ones((1,), I32), (ib[1:] != ib[:-1]).astype(I32)])
    ids = jnp.arange(N_EXPERTS, dtype=I32)
    used = cnt > 0
    later = (ids[None, :] > ids[:, None]) & used[None, :]
    nxt = jnp.min(jnp.where(later, ids[None, :], N_EXPERTS), axis=1)
    nxt = jnp.where(nxt == N_EXPERTS, -1, nxt)
    slot = (jnp.cumsum(used.astype(I32)) - used.astype(I32)) % 2
    items = tuple(a.astype(I32) for a in (ie, ib, ilo, ihi, ifirst, nxt[ie], slot[ie], n_items))
    return items


def _moe(xn, route, tile_counts, w1, w3, w2, layer):
    t = xn.shape[0]
    counts = jnp.sum(tile_counts.reshape(-1, 8, LANES)[:, 0, :], axis=0)
    start = (jnp.cumsum(counts) - counts).reshape(1, LANES)
    dest = _moe_rank(route, start)
    nt_r, _, tm_r = dest.shape
    dest = dest.reshape(nt_r, 2, tm_r // TM_TOK, TM_TOK).transpose(0, 2, 1, 3).reshape(-1)
    items = _moe_plan(counts, t)
    bufx = _moe_dispatch(xn, dest)
    bufy = _moe_experts(bufx, items, w1, w3, w2, layer)
    return bufy, dest


def _ple_kernel(dest_ref, dest_next_ref, h_ref, route_ref, p_ref, bufy_ref, pleg_ref, wg_ref, wp_ref,
                fing_ref, out_ref, y_s, sem, *, final):
    tm = h_ref.shape[0]
    i = pl.program_id(0)
    last = pl.num_programs(0) - 1
    slot = lax.rem(i, 2)
    rs = ROW_SLABS

    def y_base(s, k):
        base = (s * 2 + k) * (tm * rs)
        return base if isinstance(base, int) else pl.multiple_of(base, tm * rs)

    def row_tile_copy(src_row, s, k, t):
        src = bufy_ref.at[pl.ds(pl.multiple_of(src_row * rs, rs), rs)]
        return pltpu.make_async_copy(src, y_s.at[pl.ds(y_base(s, k) + t * rs, rs)], sem.at[s])

    def gather(dref, s):
        for t in range(tm):
            for k in range(2):
                row_tile_copy(dref[k * tm + t], s, k, t).start(priority=k)

    def drain(s):
        for _ in range(2 * tm):
            row_tile_copy(0, s, 0, 0).wait()

    @pl.when(i == 0)
    def _():
        gather(dest_ref, 0)

    drain(slot)
    gather(dest_next_ref, 1 - slot)

    r = route_ref[...]
    g0, g1 = r[:, R_G0:R_G0 + 1], r[:, R_G1:R_G1 + 1]

    def slab(k, j):
        return y_s[pl.ds(y_base(slot, k) + j, tm, stride=rs), :]

    moe = jnp.concatenate([slab(0, j) * g0 + slab(1, j) * g1 for j in range(rs)], axis=1)
    h = h_ref[...] + moe
    gate = jax.nn.sigmoid(_dot(_rms(h, pleg_ref[...]).astype(BF16), wg_ref[...]))
    h = h + gate * _dot(p_ref[...].astype(BF16), wp_ref[...])
    out_ref[...] = _rms(h, fing_ref[...]) if final else h

    @pl.when(i == last)
    def _():
        drain(1 - slot)


def _moe_combine_ple(h, route, bufy, dest, p_all, layer, ple_g, w_gate, w_proj, final_g, final):
    t = h.shape[0]
    tm = TM_TOK
    nt = t // tm
    row_spec = pl.BlockSpec((tm, D_MODEL), lambda i: (i, 0))
    return pl.pallas_call(
        functools.partial(_ple_kernel, final=final),
        grid=(nt,),
        in_specs=[pl.BlockSpec((2 * tm,), lambda i: (i,), memory_space=pltpu.SMEM),
                  pl.BlockSpec((2 * tm,), lambda i: (jnp.minimum(i + 1, nt - 1),), memory_space=pltpu.SMEM),
                  row_spec,
                  pl.BlockSpec((tm, LANES), lambda i: (i, 0)),
                  pl.BlockSpec((None, tm, PLE_DIM), lambda i: (layer, i, 0)),
                  pl.BlockSpec(memory_space=pl.ANY),
                  _const_spec((1, D_MODEL)),
                  _const_spec((D_MODEL, D_MODEL)),
                  _const_spec((PLE_DIM, D_MODEL)),
                  _const_spec((1, D_MODEL))],
        out_specs=row_spec,
        out_shape=jax.ShapeDtypeStruct((t, D_MODEL), F32),
        scratch_shapes=[pltpu.VMEM((2 * 2 * tm * ROW_SLABS, LANES), F32), pltpu.SemaphoreType.DMA((2,))],
        compiler_params=_cparams("arbitrary"),
        name="moe_combine_ple",
    )(dest, dest, h, route, p_all, bufy, ple_g.reshape(1, -1), w_gate.astype(BF16), w_proj.astype(BF16),
      final_g.reshape(1, -1))


_P_Q = 0
_P_K = _P_Q + N_HEADS * HEAD_DIM
_P_V = _P_K + KV_DIM
_P_QI = _P_V + KV_DIM
_P_KI = _P_QI + IDX_HEADS * LANES
_P_WI = _P_KI + LANES
_P_END = _P_WI + LANES


def _rope_tables(pos, inv_ref, half):
    tm = pos.shape[0]
    lane = lax.broadcasted_iota(I32, (tm, LANES), 1)
    ang = pos * inv_ref[...]
    rot = lane < 2 * half
    cos = jnp.where(rot, jnp.cos(ang), 1.0)
    sin = jnp.sin(ang)
    s_lo = jnp.where(lane < half, -sin, 0.0)
    s_hi = jnp.where(rot & (lane >= half), sin, 0.0)
    return cos, s_lo, s_hi


def _rope(x, tables, half):
    cos, s_lo, s_hi = tables
    return x * cos + pltpu.roll(x, LANES - half, 1) * s_lo + pltpu.roll(x, half, 1) * s_hi


def _dsa_in_kernel(h_ref, g_ref, pos_ref, w_ref, invq_ref, invi_ref,
                   q_ref, k_ref, v_ref, qi_ref, ki_ref, wi_ref, hn_s):
    hn_s[...] = _rms(h_ref[...], g_ref[...]).astype(BF16)
    pos = pos_ref[...].astype(F32)
    tq = _rope_tables(pos, invq_ref, ROT_DIM // 2)
    ti = _rope_tables(pos, invi_ref, IDX_ROT_DIM // 2)

    def proj(c0, width):
        return _dot(hn_s[...], w_ref[:, c0:c0 + width])

    def rope_blocks(z, out_ref, c_out, tables, half):
        for b in range(z.shape[1] // LANES):
            blk = _rope(z[:, b * LANES:(b + 1) * LANES], tables, half)
            out_ref[:, c_out + b * LANES:c_out + (b + 1) * LANES] = blk.astype(BF16)

    nw = 512
    for c in range(0, N_HEADS * HEAD_DIM, nw):
        rope_blocks(proj(_P_Q + c, nw), q_ref, c, tq, ROT_DIM // 2)
    kv = proj(_P_K, 2 * KV_DIM)
    rope_blocks(kv[:, :KV_DIM], k_ref, 0, tq, ROT_DIM // 2)
    v_ref[...] = kv[:, KV_DIM:].astype(BF16)
    for c in range(0, IDX_HEADS * LANES, nw):
        rope_blocks(proj(_P_QI + c, nw), qi_ref, c, ti, IDX_ROT_DIM // 2)
    kw = proj(_P_KI, 2 * LANES)
    rope_blocks(kw[:, :LANES], ki_ref, 0, ti, IDX_ROT_DIM // 2)
    wi_ref[...] = kw[:, LANES:] * (IDX_HEADS ** -0.5 * IDX_DIM ** -0.5)


def _inv_freq_lanes(rot_dim):
    half = rot_dim // 2
    inv = ROPE_THETA ** (-jnp.arange(0, rot_dim, 2, dtype=F32) / rot_dim)
    lanes = jnp.zeros((LANES,), F32).at[:rot_dim].set(jnp.concatenate([inv, inv]))
    del half
    return lanes.reshape(1, LANES)


def _dsa_in_weights(w_in):
    d = w_in.shape[0]
    c_q, c_k, c_v = N_HEADS * HEAD_DIM, KV_DIM, KV_DIM
    c_qi, c_ki = IDX_HEADS * IDX_DIM, IDX_DIM
    o = np.cumsum([0, c_q, c_k, c_v, c_qi, c_ki, IDX_HEADS])
    wq, wk, wv, wqi, wki, wwi = (w_in[:, o[i]:o[i + 1]] for i in range(6))
    wqi = jnp.pad(wqi.reshape(d, IDX_HEADS, IDX_DIM), ((0, 0), (0, 0), (0, LANES - IDX_DIM)))
    wki = jnp.pad(wki, ((0, 0), (0, LANES - IDX_DIM)))
    wwi = jnp.pad(wwi, ((0, 0), (0, LANES - IDX_HEADS)))
    return jnp.concatenate([wq, wk, wv, wqi.reshape(d, IDX_HEADS * LANES), wki, wwi], axis=1).astype(BF16)


def _dsa_in(h, mix_g, positions, w_in):
    t = h.shape[0]
    tm = min(TM_MIX, t)
    spec = lambda w: pl.BlockSpec((tm, w), lambda i: (i, 0))
    widths = (N_HEADS * HEAD_DIM, KV_DIM, KV_DIM, IDX_HEADS * LANES, LANES)
    return pl.pallas_call(
        _dsa_in_kernel,
        grid=(t // tm,),
        in_specs=[spec(D_MODEL), _const_spec((1, D_MODEL)), spec(1), _const_spec((D_MODEL, _P_END)),
                  _const_spec((1, LANES)), _const_spec((1, LANES))],
        out_specs=[spec(w) for w in widths] + [spec(LANES)],
        out_shape=[jax.ShapeDtypeStruct((t, w), BF16) for w in widths]
        + [jax.ShapeDtypeStruct((t, LANES), F32)],
        scratch_shapes=[pltpu.VMEM((tm, D_MODEL), BF16)],
        compiler_params=_cparams("parallel"),
        name="dsa_in_proj",
    )(h, mix_g.reshape(1, -1), positions.reshape(t, 1), _dsa_in_weights(w_in),
      _inv_freq_lanes(ROT_DIM), _inv_freq_lanes(IDX_ROT_DIM))


_INT_MIN = -2 ** 31
_U_NEG_INF = 0x007FFFFF


def _key_to_f32(u):
    bits = jnp.where(u < 0, u ^ _INT_MIN, ~u)
    return lax.bitcast_convert_type(bits, F32)


def _dsa_core_kernel(q_ref, qi_ref, wi_ref, k_ref, v_ref, ki_ref, o_ref,
                     score_s, sel_s, bias_s, lg_s, mx_s, l_s, acc_s, *, k_top):
    qb = q_ref.shape[0]
    sk = k_ref.shape[0]
    n_fold = K_TILE // LANES
    q_end = (pl.program_id(1) + 1) * qb
    n_kt = (q_end + (K_TILE - 1)) // K_TILE
    neg_inf = jnp.float32(-jnp.inf)

    def tile_start(kt):
        return pl.multiple_of(kt * K_TILE, K_TILE)

    fold_rows = 64

    def key_fold(x, op):
        return op(x.reshape(K_TILE // fold_rows, fold_rows, qb), axis=0)

    def kq_pos(k0):
        shape = (K_TILE, qb)
        return (k0 + lax.broadcasted_iota(I32, shape, 0),
                q_end - qb + lax.broadcasted_iota(I32, shape, 1))

    qi_all = jnp.concatenate([qi_ref[:, hd * LANES:(hd + 1) * LANES] for hd in range(IDX_HEADS)], axis=0)
    wi_t = wi_ref[...].T
    wrows = [wi_t[hd:hd + 1, :] for hd in range(IDX_HEADS)]

    def score_tile(kt, c):
        k0 = tile_start(kt)
        rel = _dot_nt(ki_ref[pl.ds(k0, K_TILE), :], qi_all)
        acc = jnp.zeros((K_TILE, qb), F32)
        for hd in range(IDX_HEADS):
            acc = acc + wrows[hd] * jnp.maximum(rel[:, hd * qb:(hd + 1) * qb], 0.0)
        kpos, qpos = kq_pos(k0)
        score_s[pl.ds(k0, K_TILE), :] = jnp.where(kpos <= qpos, acc, neg_inf)
        return c

    lax.fori_loop(0, n_kt, score_tile, 0)

    def count_ge(thr):
        def body(kt, acc):
            tile = score_s[pl.ds(tile_start(kt), K_TILE), :]
            return acc + key_fold(jnp.where(tile >= thr, 1.0, 0.0), jnp.sum)

        acc = lax.fori_loop(0, n_kt, body, jnp.zeros((fold_rows, qb), F32))
        return jnp.sum(acc, axis=0, keepdims=True)

    kf = jnp.float32(k_top)

    def bisect(p, ans):
        bit = lax.shift_left(jnp.int32(1), 31 - p)
        cand = ans | bit
        force = (cand >= 0) & (cand <= _U_NEG_INF)
        ok = (count_ge(_key_to_f32(cand)) >= kf) | force
        return jnp.where(ok, cand, ans)

    ans = lax.fori_loop(0, 32, bisect, jnp.zeros((1, qb), I32))
    thr = _key_to_f32(jnp.where(ans == _U_NEG_INF, ans + 1, ans))
    thr_up = _key_to_f32(ans + 1)

    def count_bin(kt, accs):
        tile = score_s[pl.ds(tile_start(kt), K_TILE), :]
        return (accs[0] + key_fold(jnp.where(tile >= thr_up, 1.0, 0.0), jnp.sum),
                accs[1] + key_fold(jnp.where(tile >= thr, 1.0, 0.0), jnp.sum))

    zero = jnp.zeros((fold_rows, qb), F32)
    above, at_least = lax.fori_loop(0, n_kt, count_bin, (zero, zero))
    c_hi = jnp.sum(above, axis=0, keepdims=True)
    need_bin = kf - c_hi
    in_bin = jnp.sum(at_least, axis=0, keepdims=True) - c_hi
    fits = in_bin <= need_bin
    thr_take = jnp.where(fits, thr, thr_up)
    need0 = jnp.where(fits, 0.0, need_bin)
    rem0 = jnp.where(fits, 0.0, in_bin)

    def init_sel(kt, c):
        k0 = tile_start(kt)
        sel_s[pl.ds(k0, K_TILE), :] = jnp.where(score_s[pl.ds(k0, K_TILE), :] >= thr_take, 0.0, NEG_BIG)
        return c

    lax.fori_loop(0, n_kt, init_sel, 0)

    def active_of(need, rem):
        return (need > 0.0) & (rem > 0.0)

    def bin_cond(carry):
        return carry[2] > 0.0

    def bin_body(carry):
        need, rem, _ = carry
        active = active_of(need, rem)

        def bin_tile(k0):
            s = score_s[pl.ds(k0, K_TILE), :]
            return s, (s >= thr) & (sel_s[pl.ds(k0, K_TILE), :] != 0.0)

        def best_of(kt, best):
            s, in_bin = bin_tile(tile_start(kt))
            return jnp.maximum(best, key_fold(jnp.where(in_bin, s, neg_inf), jnp.max))

        best = lax.fori_loop(0, n_kt, best_of, jnp.full((fold_rows, qb), neg_inf, F32))
        m = jnp.max(best, axis=0, keepdims=True)

        def first_of(kt, first):
            k0 = tile_start(kt)
            s, in_bin = bin_tile(k0)
            kpos, _ = kq_pos(k0)
            return jnp.minimum(first, key_fold(jnp.where((s == m) & in_bin, kpos, sk), jnp.min))

        first = lax.fori_loop(0, n_kt, first_of, jnp.full((fold_rows, qb), sk, I32))
        pick = jnp.min(first, axis=0, keepdims=True)

        def mark(kt, c):
            k0 = tile_start(kt)
            kpos, _ = kq_pos(k0)
            sel_s[pl.ds(k0, K_TILE), :] = jnp.where(active & (kpos == pick), 0.0, sel_s[pl.ds(k0, K_TILE), :])
            return c

        lax.fori_loop(0, n_kt, mark, 0)
        step = jnp.where(active, 1.0, 0.0)
        need, rem = need - step, rem - step
        go = jnp.max(jnp.where(active_of(need, rem), 1.0, 0.0))
        return need, rem, go

    go0 = jnp.max(jnp.where(active_of(need0, rem0), 1.0, 0.0))
    lax.while_loop(bin_cond, bin_body, (need0, rem0, go0))

    def to_rows(kt, c):
        k0 = tile_start(kt)
        bias_s[:, pl.ds(k0, K_TILE)] = sel_s[pl.ds(k0, K_TILE), :].T
        return c

    lax.fori_loop(0, n_kt, to_rows, 0)

    scale = HEAD_DIM ** -0.5 * float(np.log2(np.e))

    def q_rows(g):
        return jnp.concatenate(
            [q_ref[:, (g * HEADS_PER_KV + r) * HEAD_DIM:(g * HEADS_PER_KV + r + 1) * HEAD_DIM]
             for r in range(HEADS_PER_KV)], axis=0)

    def fold(x, op):
        out = x[:, 0:LANES]
        for c in range(1, n_fold):
            out = op(out, x[:, c * LANES:(c + 1) * LANES])
        return out

    mx_s[...] = jnp.full(mx_s.shape, NEG_BIG, F32)
    l_s[...] = jnp.zeros(l_s.shape, F32)
    acc_s[...] = jnp.zeros(acc_s.shape, F32)

    def logit_tile(kt, c):
        k0 = tile_start(kt)
        bias = jnp.concatenate([bias_s[:, pl.ds(k0, K_TILE)]] * HEADS_PER_KV, axis=0)
        for g in range(N_KV_HEADS):
            kt_ = k_ref[pl.ds(k0, K_TILE), g * HEAD_DIM:(g + 1) * HEAD_DIM]
            logits = _dot_nt(q_rows(g), kt_) * scale + bias
            lg_s[g, :, pl.ds(k0, K_TILE)] = logits
            mx_s[g] = jnp.maximum(mx_s[g], fold(logits, jnp.maximum))
        return c

    lax.fori_loop(0, n_kt, logit_tile, 0)
    row_max = [jnp.max(mx_s[g], axis=-1, keepdims=True) for g in range(N_KV_HEADS)]

    def att_tile(kt, c):
        k0 = tile_start(kt)
        for g in range(N_KV_HEADS):
            vt_ = v_ref[pl.ds(k0, K_TILE), g * HEAD_DIM:(g + 1) * HEAD_DIM]
            pexp = jnp.exp2(lg_s[g, :, pl.ds(k0, K_TILE)] - row_max[g])
            l_s[g] = l_s[g] + fold(pexp, jnp.add)
            acc_s[g] = acc_s[g] + _dot(pexp.astype(BF16), vt_)
        return c

    lax.fori_loop(0, n_kt, att_tile, 0)
    for g in range(N_KV_HEADS):
        og = acc_s[g] / jnp.sum(l_s[g], axis=-1, keepdims=True)
        for r in range(HEADS_PER_KV):
            c0 = (g * HEADS_PER_KV + r) * HEAD_DIM
            o_ref[:, c0:c0 + HEAD_DIM] = og[r * qb:(r + 1) * qb, :].astype(BF16)


def _dsa_core(q, k, v, qi, ki, wi, batch, seq):
    qb = Q_BLOCK
    assert qb == LANES and seq % K_TILE == 0
    k_top = min(TOPK_MAX, seq // 4)
    view = lambda a: a.reshape(batch, seq, a.shape[-1])
    q3, k3, v3, qi3, ki3, wi3 = (view(a) for a in (q, k, v, qi, ki, wi))
    qspec = lambda w: pl.BlockSpec((None, qb, w), lambda b, j: (b, j, 0))
    kspec = lambda w: pl.BlockSpec((None, seq, w), lambda b, j: (b, 0, 0))
    rows = HEADS_PER_KV * qb
    out = pl.pallas_call(
        functools.partial(_dsa_core_kernel, k_top=k_top),
        grid=(batch, seq // qb),
        in_specs=[qspec(D_MODEL), qspec(IDX_HEADS * LANES), qspec(LANES),
                  kspec(KV_DIM), kspec(KV_DIM), kspec(LANES)],
        out_specs=qspec(D_MODEL),
        out_shape=jax.ShapeDtypeStruct((batch, seq, D_MODEL), BF16),
        scratch_shapes=[pltpu.VMEM((seq, qb), F32), pltpu.VMEM((seq, qb), F32),
                        pltpu.VMEM((qb, seq), F32),
                        pltpu.VMEM((N_KV_HEADS, rows, seq), F32),
                        pltpu.VMEM((N_KV_HEADS, rows, LANES), F32),
                        pltpu.VMEM((N_KV_HEADS, rows, LANES), F32),
                        pltpu.VMEM((N_KV_HEADS, rows, HEAD_DIM), F32)],
        compiler_params=_cparams("parallel", "arbitrary"),
        name="dsa_core",
    )(q3, qi3, wi3, k3, v3, ki3)
    return out.reshape(batch * seq, D_MODEL)


def _dsa_out_kernel(h_ref, a_ref, wo_ref, ffng_ref, wr_ref, br_ref, h_out_ref, xn_ref, route_ref,
                    cnt_ref):
    h = h_ref[...] + _dot(a_ref[...], wo_ref[...])
    h_out_ref[...] = h
    xn = _rms(h, ffng_ref[...])
    xn_ref[...] = xn
    route_ref[...], cnt_ref[...] = _route(xn, wr_ref, br_ref)


def _dsa_out(h, att, w_out, ffn_g, wr, br):
    t = h.shape[0]
    tm = min(TM_MIX, t)
    row_spec = pl.BlockSpec((tm, D_MODEL), lambda i: (i, 0))
    return pl.pallas_call(
        _dsa_out_kernel,
        grid=(t // tm,),
        in_specs=[row_spec, row_spec, _const_spec((D_MODEL, D_MODEL)), _const_spec((1, D_MODEL)),
                  _const_spec((D_MODEL, LANES)), _const_spec((1, LANES))],
        out_specs=[row_spec, row_spec, pl.BlockSpec((tm, LANES), lambda i: (i, 0)),
                   pl.BlockSpec((8, LANES), lambda i: (i, 0))],
        out_shape=[jax.ShapeDtypeStruct((t, D_MODEL), F32), jax.ShapeDtypeStruct((t, D_MODEL), F32),
                   jax.ShapeDtypeStruct((t, LANES), F32), jax.ShapeDtypeStruct((t // tm * 8, LANES), F32)],
        compiler_params=_cparams("parallel"),
        name="dsa_out_proj",
    )(h, att, w_out.astype(BF16), ffn_g.reshape(1, -1), wr, br)


def kernel(x, p, positions, mix_norm, a_w_in, a_v_gain, a_w_s, a_b_s, a_w_out, b_w_in, b_w_out,
           ffn_norm, moe_w_group, moe_b_group, moe_w_expert, moe_b_expert, moe_w1, moe_w3, moe_w2,
           ple_norm, ple_w_gate, ple_w_proj, final_norm):
    batch, seq, d = x.shape
    depth = mix_norm.shape[0]
    t = batch * seq
    h = x.reshape(t, d)
    for i in range(depth):
        j = i // 2
        wr, br = _router_weights(moe_w_group[i], moe_b_group[i], moe_w_expert[i], moe_b_expert[i])
        if i % 2 == 0:
            h, xn, route, tile_counts = _gmlp_layer(h, mix_norm[i], a_w_in[j], a_v_gain[j], a_w_s[j],
                                                    a_b_s[j], a_w_out[j], ffn_norm[i], wr, br)
        else:
            q, k, v, qi, ki, wi = _dsa_in(h, mix_norm[i], positions, b_w_in[j])
            att = _dsa_core(q, k, v, qi, ki, wi, batch, seq)
            h, xn, route, tile_counts = _dsa_out(h, att, b_w_out[j], ffn_norm[i], wr, br)
        bufy, dest = _moe(xn, route, tile_counts, moe_w1, moe_w3, moe_w2, i)
        h = _moe_combine_ple(h, route, bufy, dest, p.reshape(depth, t, -1), i, ple_norm[i],
                             ple_w_gate[i], ple_w_proj[i], final_norm, final=(i == depth - 1))
    return h.reshape(batch, seq, d)
```

```python
import functools

import numpy as np
import jax
import jax.numpy as jnp
from jax import lax
from jax.experimental import pallas as pl
from jax.experimental.pallas import tpu as pltpu

F32 = jnp.float32
BF16 = jnp.bfloat16
I32 = jnp.int32

D_MODEL = 1024
EPS = 1e-6
ROPE_THETA = 500000.0
GM_CHUNK = 128
GM_WIDTH = 2 * D_MODEL
GM_GROUPS = 8
GM_GROUP_DIM = GM_WIDTH // GM_GROUPS
N_HEADS = 8
HEAD_DIM = D_MODEL // N_HEADS
N_KV_HEADS = 2
HEADS_PER_KV = N_HEADS // N_KV_HEADS
KV_DIM = N_KV_HEADS * HEAD_DIM
ROT_DIM = HEAD_DIM // 4
IDX_HEADS = 8
IDX_DIM = 64
IDX_ROT_DIM = IDX_DIM // 4
TOPK_MAX = 256
N_EXPERT_GROUPS = 4
EXPERTS_PER_GROUP = 8
N_EXPERTS = N_EXPERT_GROUPS * EXPERTS_PER_GROUP
D_EXPERT = D_MODEL // 2
PLE_DIM = 256

LANES = 128
V7X_VMEM_LIMIT = 56 * 1024 * 1024

TM_MIX = 512
TM_TOK = 256
TM_RANK = 512
MOE_ROWS = 256
Q_BLOCK = 128
K_TILE = 512
NEG_BIG = -1e30

R_E0, R_E1, R_G0, R_G1 = 0, 1, 2, 3
R_EXP0 = N_EXPERT_GROUPS


def _cparams(*sem):
    return pltpu.CompilerParams(dimension_semantics=sem, vmem_limit_bytes=V7X_VMEM_LIMIT)


def _const_spec(shape):
    nd = len(shape)
    return pl.BlockSpec(shape, lambda *_: (0,) * nd, pipeline_mode=pl.Buffered(1))


def _rms(x, g):
    return x * lax.rsqrt(jnp.mean(x * x, axis=-1, keepdims=True) + EPS) * g


def _gelu(x):
    return jax.nn.gelu(x)


def _dot(a, b):
    return jnp.dot(a, b, preferred_element_type=F32)


def _dot_nt(a, b):
    return lax.dot_general(a, b, (((1,), (1,)), ((), ())), preferred_element_type=F32)


def _split_bf16(a):
    hi = a.astype(BF16)
    lo = (a - hi.astype(F32)).astype(BF16)
    return hi, lo


def _dot_3pass(a, b):
    ah, al = _split_bf16(a)
    bh, bl = _split_bf16(b)
    return _dot(ah, bh) + (_dot(al, bh) + _dot(ah, bl))


def _route(xn, wr_ref, br_ref):
    tm = xn.shape[0]
    logits = _dot_3pass(xn, wr_ref[...]) + br_ref[...]
    lane = lax.broadcasted_iota(I32, (tm, LANES), 1)
    neg_inf = jnp.float32(-jnp.inf)

    gmask = lane < N_EXPERT_GROUPS
    gl = jnp.where(gmask, logits, neg_inf)
    gmax = jnp.max(gl, axis=-1, keepdims=True)
    g_sel = jnp.min(jnp.where(gl == gmax, lane, LANES), axis=-1, keepdims=True)
    p_group = 1.0 / jnp.sum(jnp.where(gmask, jnp.exp(gl - gmax), 0.0), axis=-1, keepdims=True)

    lo = R_EXP0 + EXPERTS_PER_GROUP * g_sel
    lmask = (lane >= lo) & (lane < lo + EXPERTS_PER_GROUP)
    ll = jnp.where(lmask, logits, neg_inf)
    lmax = jnp.max(ll, axis=-1, keepdims=True)
    le = jnp.where(lmask, jnp.exp(ll - lmax), 0.0)
    probs = jnp.where(lmask, le / jnp.sum(le, axis=-1, keepdims=True), -1.0)
    p1 = jnp.max(probs, axis=-1, keepdims=True)
    i1 = jnp.min(jnp.where(probs == p1, lane, LANES), axis=-1, keepdims=True)
    probs2 = jnp.where(lane == i1, -1.0, probs)
    p2 = jnp.max(probs2, axis=-1, keepdims=True)
    i2 = jnp.min(jnp.where(probs2 == p2, lane, LANES), axis=-1, keepdims=True)
    denom = p1 + p2
    g1 = p_group * p1 / denom
    g2 = p_group * p2 / denom
    e1 = (i1 - R_EXP0).astype(F32)
    e2 = (i2 - R_EXP0).astype(F32)
    out = jnp.where(lane == R_E0, e1, 0.0)
    out = jnp.where(lane == R_E1, e2, out)
    out = jnp.where(lane == R_G0, g1, out)
    out = jnp.where(lane == R_G1, g2, out)
    hits = jnp.where((lane == i1 - R_EXP0) | (lane == i2 - R_EXP0), 1.0, 0.0)
    counts = jnp.where(lax.broadcasted_iota(I32, (8, LANES), 0) == 0,
                       jnp.sum(hits, axis=0, keepdims=True), 0.0)
    return out, counts


def _router_weights(w_group, b_group, w_expert, b_expert):
    wr = jnp.zeros((D_MODEL, LANES), F32)
    wr = wr.at[:, :N_EXPERT_GROUPS].set(w_group).at[:, R_EXP0:R_EXP0 + N_EXPERTS].set(w_expert)
    br = jnp.zeros((1, LANES), F32)
    br = br.at[0, :N_EXPERT_GROUPS].set(b_group).at[0, R_EXP0:R_EXP0 + N_EXPERTS].set(b_expert)
    return wr, br


def _gmlp_kernel(x_ref, mixg_ref, win_ref, vgain_ref, ws_ref, bst_ref, wout_ref, ffng_ref, wr_ref,
                 br_ref, h_ref, xn_ref, route_ref, cnt_ref, xn_s, v_s, prod_s):
    tm = x_ref.shape[0]
    xn_s[...] = _rms(x_ref[...], mixg_ref[...]).astype(BF16)

    nt = 512
    ssq = jnp.zeros((tm, 1), F32)
    for j in range(GM_WIDTH // nt):
        z = _gelu(_dot(xn_s[...], win_ref[:, GM_WIDTH + j * nt:GM_WIDTH + (j + 1) * nt]))
        v_s[:, j * nt:(j + 1) * nt] = z
        ssq = ssq + jnp.sum(z * z, axis=-1, keepdims=True)
    rinv = lax.rsqrt(ssq * (1.0 / GM_WIDTH) + EPS)

    row = lax.broadcasted_iota(I32, (GM_CHUNK, GM_CHUNK), 0)
    col = lax.broadcasted_iota(I32, (GM_CHUNK, GM_CHUNK), 1)
    tril = row >= col
    gd = GM_GROUP_DIM
    for g in range(GM_GROUPS):
        wm = jnp.where(tril, ws_ref[g], 0.0).astype(BF16)
        vg = (v_s[:, g * gd:(g + 1) * gd] * rinv * vgain_ref[:, g * gd:(g + 1) * gd]).astype(BF16)
        u = _gelu(_dot(xn_s[...], win_ref[:, g * gd:(g + 1) * gd]))
        bias = bst_ref[:, g:g + 1]
        for c in range(tm // GM_CHUNK):
            rows = slice(c * GM_CHUNK, (c + 1) * GM_CHUNK)
            sv = _dot(wm, vg[rows, :]) + bias
            prod_s[rows, g * gd:(g + 1) * gd] = (u[rows, :] * sv).astype(BF16)

    h = x_ref[...] + _dot(prod_s[...], wout_ref[...])
    h_ref[...] = h
    xn = _rms(h, ffng_ref[...])
    xn_ref[...] = xn
    route_ref[...], cnt_ref[...] = _route(xn, wr_ref, br_ref)


def _gmlp_layer(x, mix_g, w_in, v_gain, w_s, b_s, w_out, ffn_g, wr, br):
    t = x.shape[0]
    tm = TM_MIX
    row_spec = pl.BlockSpec((tm, D_MODEL), lambda i: (i, 0))
    return pl.pallas_call(
        _gmlp_kernel,
        grid=(t // tm,),
        in_specs=[
            row_spec,
            _const_spec((1, D_MODEL)),
            _const_spec((D_MODEL, 2 * GM_WIDTH)),
            _const_spec((1, GM_WIDTH)),
            _const_spec((GM_GROUPS, GM_CHUNK, GM_CHUNK)),
            _const_spec((GM_CHUNK, GM_GROUPS)),
            _const_spec((GM_WIDTH, D_MODEL)),
            _const_spec((1, D_MODEL)),
            _const_spec((D_MODEL, LANES)),
            _const_spec((1, LANES)),
        ],
        out_specs=[row_spec, row_spec, pl.BlockSpec((tm, LANES), lambda i: (i, 0)),
                   pl.BlockSpec((8, LANES), lambda i: (i, 0))],
        out_shape=[jax.ShapeDtypeStruct((t, D_MODEL), F32), jax.ShapeDtypeStruct((t, D_MODEL), F32),
                   jax.ShapeDtypeStruct((t, LANES), F32), jax.ShapeDtypeStruct((t // tm * 8, LANES), F32)],
        scratch_shapes=[pltpu.VMEM((tm, D_MODEL), BF16), pltpu.VMEM((tm, GM_WIDTH), F32),
                        pltpu.VMEM((tm, GM_WIDTH), BF16)],
        compiler_params=_cparams("parallel"),
        name="gmlp_mixer",
    )(x, mix_g.reshape(1, -1), w_in.astype(BF16), v_gain.reshape(1, -1), w_s, b_s.T,
      w_out.astype(BF16), ffn_g.reshape(1, -1), wr, br)


def _rank_kernel(route_ref, start_ref, dest_ref, carry_s):
    tm = route_ref.shape[0]

    @pl.when(pl.program_id(0) == 0)
    def _():
        carry_s[...] = jnp.zeros_like(carry_s)

    r = route_ref[...]
    lane = lax.broadcasted_iota(I32, (tm, LANES), 1)
    lanef = lane.astype(F32)
    oh0 = (lanef == r[:, R_E0:R_E0 + 1]).astype(F32)
    oh1 = (lanef == r[:, R_E1:R_E1 + 1]).astype(F32)
    both = oh0 + oh1
    row = lax.broadcasted_iota(I32, (tm, tm), 0)
    col = lax.broadcasted_iota(I32, (tm, tm), 1)
    strict_lower = (row > col).astype(BF16)
    base = start_ref[...] + carry_s[0:1, :] + _dot(strict_lower, both.astype(BF16))
    d0 = jnp.sum(oh0 * base, axis=-1, keepdims=True)
    d1 = jnp.sum(oh1 * (base + oh0), axis=-1, keepdims=True)
    cols = jnp.where(lane == 0, d0, jnp.where(lane == 1, d1, 0.0))
    dest_ref[...] = cols.T[0:8, :].astype(I32)
    carry_s[0:1, :] = carry_s[0:1, :] + jnp.sum(both, axis=0, keepdims=True)


def _moe_rank(route, start):
    t = route.shape[0]
    tm = min(TM_RANK, t)
    nt = t // tm
    dest = pl.pallas_call(
        _rank_kernel,
        grid=(nt,),
        in_specs=[pl.BlockSpec((tm, LANES), lambda i: (i, 0)), _const_spec((1, LANES))],
        out_specs=pl.BlockSpec((8, tm), lambda i: (i, 0)),
        out_shape=jax.ShapeDtypeStruct((nt * 8, tm), I32),
        scratch_shapes=[pltpu.VMEM((8, LANES), F32)],
        compiler_params=_cparams("arbitrary"),
        name="moe_rank",
    )(route, start)
    return dest.reshape(nt, 8, tm)[:, :2, :]


def _row_copy(src_ref, dst_ref, sem, s, d):
    return pltpu.make_async_copy(src_ref.at[pl.ds(s, 1)], dst_ref.at[pl.ds(d, 1)], sem)


_DISPATCH_BUFS = 3


def _dispatch_kernel(dest_ref, xn_ref, bufx_ref, x_s, load_sem, row_sem):
    tm = TM_TOK
    i = pl.program_id(0)
    n = pl.num_programs(0)
    slot = lax.rem(i, _DISPATCH_BUFS)
    nxt = lax.rem(i + 1, _DISPATCH_BUFS)

    def load(tile, s):
        return pltpu.make_async_copy(xn_ref.at[pl.ds(tile * tm, tm)], x_s.at[s], load_sem.at[s])

    def drain_rows(s):
        for _ in range(2 * tm):
            _row_copy(x_s.at[s], bufx_ref, row_sem.at[s], 0, 0).wait()

    @pl.when(i == 0)
    def _():
        load(0, 0).start()

    @pl.when(i >= 2)
    def _():
        drain_rows(nxt)

    @pl.when(i + 1 < n)
    def _():
        load(i + 1, nxt).start()

    load(i, slot).wait()
    for t in range(tm):
        for k in range(2):
            _row_copy(x_s.at[slot], bufx_ref, row_sem.at[slot], t, dest_ref[k * tm + t]).start(priority=k)

    @pl.when(i == n - 1)
    def _():
        drain_rows(slot)

    @pl.when((i == n - 1) & (i >= 1))
    def _():
        drain_rows(lax.rem(i + 2, _DISPATCH_BUFS))


def _moe_dispatch(xn, dest_flat):
    t = xn.shape[0]
    tm = TM_TOK
    return pl.pallas_call(
        _dispatch_kernel,
        grid=(t // tm,),
        in_specs=[pl.BlockSpec((2 * tm,), lambda i: (i,), memory_space=pltpu.SMEM),
                  pl.BlockSpec(memory_space=pl.ANY)],
        out_specs=pl.BlockSpec(memory_space=pl.ANY),
        out_shape=jax.ShapeDtypeStruct((2 * t, D_MODEL), F32),
        scratch_shapes=[pltpu.VMEM((_DISPATCH_BUFS, tm, D_MODEL), F32),
                        pltpu.SemaphoreType.DMA((_DISPATCH_BUFS,)),
                        pltpu.SemaphoreType.DMA((_DISPATCH_BUFS,))],
        compiler_params=_cparams("arbitrary"),
        name="moe_dispatch",
    )(dest_flat, xn)


ROW_SLABS = D_MODEL // LANES


def _expert_kernel(ie_ref, ib_ref, ilo_ref, ihi_ref, ifirst_ref, inext_ref, islot_ref, n_ref,
                   x_ref, w1_ref, w3_ref, w2_ref, y_ref, wf1_s, wf3_s, wf2_s, w1_s, w3_s, w2_s, wsem,
                   *, layer):
    w = pl.program_id(0)
    br = x_ref.shape[0]
    live = w < n_ref[0]
    e = ie_ref[w]
    prev = ie_ref[jnp.maximum(w - 1, 0)]
    slot = islot_ref[w]

    def weight_copies(expert, s):
        return [pltpu.make_async_copy(src.at[layer, expert], dst.at[s], wsem.at[s, i])
                for i, (src, dst) in enumerate(((w1_ref, wf1_s), (w3_ref, wf3_s), (w2_ref, wf2_s)))]

    @pl.when(live & (w == 0))
    def _():
        for c in weight_copies(e, slot):
            c.start()

    @pl.when(live & ((w == 0) | (e != prev)))
    def _():
        for c in weight_copies(e, slot):
            c.wait()

        @pl.when(inext_ref[w] >= 0)
        def _():
            for c in weight_copies(inext_ref[w], 1 - slot):
                c.start()

        w1_s[...] = wf1_s[slot].astype(BF16)
        w3_s[...] = wf3_s[slot].astype(BF16)
        w2_s[...] = wf2_s[slot].astype(BF16)

    def ffn():
        rows = lax.broadcasted_iota(I32, (br, 1), 0)
        mine = (rows >= ilo_ref[w]) & (rows < ihi_ref[w])
        x = jnp.where(mine, x_ref[...], 0.0).astype(BF16)
        a = _dot(x, w1_s[...])
        b = _dot(x, w3_s[...])
        mid = (a * jax.nn.sigmoid(a) * b).astype(BF16)
        return _dot(mid, w2_s[...])

    @pl.when(live & (ifirst_ref[w] == 1))
    def _():
        y = ffn()
        for j in range(ROW_SLABS):
            y_ref[pl.ds(j, br, stride=ROW_SLABS), :] = y[:, j * LANES:(j + 1) * LANES]

    @pl.when(live & (ifirst_ref[w] == 0))
    def _():
        y = ffn()
        for j in range(ROW_SLABS):
            rows_j = pl.ds(j, br, stride=ROW_SLABS)
            y_ref[rows_j, :] = y_ref[rows_j, :] + y[:, j * LANES:(j + 1) * LANES]


def _moe_experts(bufx, items, w1, w3, w2, layer):
    n_rows = bufx.shape[0]
    br = MOE_ROWS
    n_items_max = items[0].shape[0]
    n_prefetch = len(items)

    def row_map(w, *pref):
        return (pref[1][w], 0)

    grid_spec = pltpu.PrefetchScalarGridSpec(
        num_scalar_prefetch=n_prefetch,
        grid=(n_items_max,),
        in_specs=[pl.BlockSpec((br, D_MODEL), row_map),
                  pl.BlockSpec(memory_space=pl.ANY),
                  pl.BlockSpec(memory_space=pl.ANY),
                  pl.BlockSpec(memory_space=pl.ANY)],
        out_specs=pl.BlockSpec((br * ROW_SLABS, LANES), row_map),
        scratch_shapes=[pltpu.VMEM((2, D_MODEL, D_EXPERT), F32), pltpu.VMEM((2, D_MODEL, D_EXPERT), F32),
                        pltpu.VMEM((2, D_EXPERT, D_MODEL), F32),
                        pltpu.VMEM((D_MODEL, D_EXPERT), BF16), pltpu.VMEM((D_MODEL, D_EXPERT), BF16),
                        pltpu.VMEM((D_EXPERT, D_MODEL), BF16),
                        pltpu.SemaphoreType.DMA((2, 3))],
    )
    return pl.pallas_call(
        functools.partial(_expert_kernel, layer=layer),
        grid_spec=grid_spec,
        out_shape=jax.ShapeDtypeStruct((n_rows * ROW_SLABS, LANES), F32),
        compiler_params=_cparams("arbitrary"),
        name="moe_experts",
    )(*items, bufx, w1, w3, w2)


def _moe_plan(counts, t):
    br = MOE_ROWS
    nb = 2 * t // br
    n_items_max = nb + N_EXPERTS - 1
    cnt = counts[:N_EXPERTS].astype(I32)
    end = jnp.cumsum(cnt)
    start = end - cnt

    first_blk = start // br
    n_blk = jnp.where(cnt > 0, (end - 1) // br - first_blk + 1, 0)
    iend = jnp.cumsum(n_blk)
    istart = iend - n_blk
    n_items = iend[-1:]
    w = jnp.minimum(jnp.arange(n_items_max, dtype=I32), n_items[0] - 1)
    ie = jnp.minimum(jnp.sum((iend[None, :] <= w[:, None]).astype(I32), axis=1), N_EXPERTS - 1)
    ib = first_blk[ie] + (w - istart[ie])
    ilo = jnp.maximum(start[ie], ib * br) - ib * br
    ihi = jnp.minimum(end[ie], (ib + 1) * br) - ib * br
    ifirst = jnp.concatenate([jnp.ones((1,), I32), (ib[1:] != ib[:-1]).astype(I32)])
    ids = jnp.arange(N_EXPERTS, dtype=I32)
    used = cnt > 0
    later = (ids[None, :] > ids[:, None]) & used[None, :]
    nxt = jnp.min(jnp.where(later, ids[None, :], N_EXPERTS), axis=1)
    nxt = jnp.where(nxt == N_EXPERTS, -1, nxt)
    slot = (jnp.cumsum(used.astype(I32)) - used.astype(I32)) % 2
    items = tuple(a.astype(I32) for a in (ie, ib, ilo, ihi, ifirst, nxt[ie], slot[ie], n_items))
    return items


def _moe(xn, route, tile_counts, w1, w3, w2, layer):
    t = xn.shape[0]
    counts = jnp.sum(tile_counts.reshape(-1, 8, LANES)[:, 0, :], axis=0)
    start = (jnp.cumsum(counts) - counts).reshape(1, LANES)
    dest = _moe_rank(route, start)
    nt_r, _, tm_r = dest.shape
    dest = dest.reshape(nt_r, 2, tm_r // TM_TOK, TM_TOK).transpose(0, 2, 1, 3).reshape(-1)
    items = _moe_plan(counts, t)
    bufx = _moe_dispatch(xn, dest)
    bufy = _moe_experts(bufx, items, w1, w3, w2, layer)
    return bufy, dest


def _ple_kernel(dest_ref, dest_next_ref, h_ref, route_ref, p_ref, bufy_ref, pleg_ref, wg_ref, wp_ref,
                fing_ref, out_ref, y_s, sem, *, final):
    tm = h_ref.shape[0]
    i = pl.program_id(0)
    last = pl.num_programs(0) - 1
    slot = lax.rem(i, 2)
    rs = ROW_SLABS

    def y_base(s, k):
        base = (s * 2 + k) * (tm * rs)
        return base if isinstance(base, int) else pl.multiple_of(base, tm * rs)

    def row_tile_copy(src_row, s, k, t):
        src = bufy_ref.at[pl.ds(pl.multiple_of(src_row * rs, rs), rs)]
        return pltpu.make_async_copy(src, y_s.at[pl.ds(y_base(s, k) + t * rs, rs)], sem.at[s])

    def gather(dref, s):
        for t in range(tm):
            for k in range(2):
                row_tile_copy(dref[k * tm + t], s, k, t).start(priority=k)

    def drain(s):
        for _ in range(2 * tm):
            row_tile_copy(0, s, 0, 0).wait()

    @pl.when(i == 0)
    def _():
        gather(dest_ref, 0)

    drain(slot)
    gather(dest_next_ref, 1 - slot)

    r = route_ref[...]
    g0, g1 = r[:, R_G0:R_G0 + 1], r[:, R_G1:R_G1 + 1]

    def slab(k, j):
        return y_s[pl.ds(y_base(slot, k) + j, tm, stride=rs), :]

    moe = jnp.concatenate([slab(0, j) * g0 + slab(1, j) * g1 for j in range(rs)], axis=1)
    h = h_ref[...] + moe
    gate = jax.nn.sigmoid(_dot(_rms(h, pleg_ref[...]).astype(BF16), wg_ref[...]))
    h = h + gate * _dot(p_ref[...].astype(BF16), wp_ref[...])
    out_ref[...] = _rms(h, fing_ref[...]) if final else h

    @pl.when(i == last)
    def _():
        drain(1 - slot)


def _moe_combine_ple(h, route, bufy, dest, p_all, layer, ple_g, w_gate, w_proj, final_g, final):
    t = h.shape[0]
    tm = TM_TOK
    nt = t // tm
    row_spec = pl.BlockSpec((tm, D_MODEL), lambda i: (i, 0))
    return pl.pallas_call(
        functools.partial(_ple_kernel, final=final),
        grid=(nt,),
        in_specs=[pl.BlockSpec((2 * tm,), lambda i: (i,), memory_space=pltpu.SMEM),
                  pl.BlockSpec((2 * tm,), lambda i: (jnp.minimum(i + 1, nt - 1),), memory_space=pltpu.SMEM),
                  row_spec,
                  pl.BlockSpec((tm, LANES), lambda i: (i, 0)),
                  pl.BlockSpec((None, tm, PLE_DIM), lambda i: (layer, i, 0)),
                  pl.BlockSpec(memory_space=pl.ANY),
                  _const_spec((1, D_MODEL)),
                  _const_spec((D_MODEL, D_MODEL)),
                  _const_spec((PLE_DIM, D_MODEL)),
                  _const_spec((1, D_MODEL))],
        out_specs=row_spec,
        out_shape=jax.ShapeDtypeStruct((t, D_MODEL), F32),
        scratch_shapes=[pltpu.VMEM((2 * 2 * tm * ROW_SLABS, LANES), F32), pltpu.SemaphoreType.DMA((2,))],
        compiler_params=_cparams("arbitrary"),
        name="moe_combine_ple",
    )(dest, dest, h, route, p_all, bufy, ple_g.reshape(1, -1), w_gate.astype(BF16), w_proj.astype(BF16),
      final_g.reshape(1, -1))


_P_Q = 0
_P_K = _P_Q + N_HEADS * HEAD_DIM
_P_V = _P_K + KV_DIM
_P_QI = _P_V + KV_DIM
_P_KI = _P_QI + IDX_HEADS * LANES
_P_WI = _P_KI + LANES
_P_END = _P_WI + LANES


def _rope_tables(pos, inv_ref, half):
    tm = pos.shape[0]
    lane = lax.broadcasted_iota(I32, (tm, LANES), 1)
    ang = pos * inv_ref[...]
    rot = lane < 2 * half
    cos = jnp.where(rot, jnp.cos(ang), 1.0)
    sin = jnp.sin(ang)
    s_lo = jnp.where(lane < half, -sin, 0.0)
    s_hi = jnp.where(rot & (lane >= half), sin, 0.0)
    return cos, s_lo, s_hi


def _rope(x, tables, half):
    cos, s_lo, s_hi = tables
    return x * cos + pltpu.roll(x, LANES - half, 1) * s_lo + pltpu.roll(x, half, 1) * s_hi


def _dsa_in_kernel(h_ref, g_ref, pos_ref, w_ref, invq_ref, invi_ref,
                   q_ref, k_ref, v_ref, qi_ref, ki_ref, wi_ref, hn_s):
    hn_s[...] = _rms(h_ref[...], g_ref[...]).astype(BF16)
    pos = pos_ref[...].astype(F32)
    tq = _rope_tables(pos, invq_ref, ROT_DIM // 2)
    ti = _rope_tables(pos, invi_ref, IDX_ROT_DIM // 2)

    def proj(c0, width):
        return _dot(hn_s[...], w_ref[:, c0:c0 + width])

    def rope_blocks(z, out_ref, c_out, tables, half):
        for b in range(z.shape[1] // LANES):
            blk = _rope(z[:, b * LANES:(b + 1) * LANES], tables, half)
            out_ref[:, c_out + b * LANES:c_out + (b + 1) * LANES] = blk.astype(BF16)

    nw = 512
    for c in range(0, N_HEADS * HEAD_DIM, nw):
        rope_blocks(proj(_P_Q + c, nw), q_ref, c, tq, ROT_DIM // 2)
    kv = proj(_P_K, 2 * KV_DIM)
    rope_blocks(kv[:, :KV_DIM], k_ref, 0, tq, ROT_DIM // 2)
    v_ref[...] = kv[:, KV_DIM:].astype(BF16)
    for c in range(0, IDX_HEADS * LANES, nw):
        rope_blocks(proj(_P_QI + c, nw), qi_ref, c, ti, IDX_ROT_DIM // 2)
    kw = proj(_P_KI, 2 * LANES)
    rope_blocks(kw[:, :LANES], ki_ref, 0, ti, IDX_ROT_DIM // 2)
    wi_ref[...] = kw[:, LANES:] * (IDX_HEADS ** -0.5 * IDX_DIM ** -0.5)


def _inv_freq_lanes(rot_dim):
    half = rot_dim // 2
    inv = ROPE_THETA ** (-jnp.arange(0, rot_dim, 2, dtype=F32) / rot_dim)
    lanes = jnp.zeros((LANES,), F32).at[:rot_dim].set(jnp.concatenate([inv, inv]))
    del half
    return lanes.reshape(1, LANES)


def _dsa_in_weights(w_in):
    d = w_in.shape[0]
    c_q, c_k, c_v = N_HEADS * HEAD_DIM, KV_DIM, KV_DIM
    c_qi, c_ki = IDX_HEADS * IDX_DIM, IDX_DIM
    o = np.cumsum([0, c_q, c_k, c_v, c_qi, c_ki, IDX_HEADS])
    wq, wk, wv, wqi, wki, wwi = (w_in[:, o[i]:o[i + 1]] for i in range(6))
    wqi = jnp.pad(wqi.reshape(d, IDX_HEADS, IDX_DIM), ((0, 0), (0, 0), (0, LANES - IDX_DIM)))
    wki = jnp.pad(wki, ((0, 0), (0, LANES - IDX_DIM)))
    wwi = jnp.pad(wwi, ((0, 0), (0, LANES - IDX_HEADS)))
    return jnp.concatenate([wq, wk, wv, wqi.reshape(d, IDX_HEADS * LANES), wki, wwi], axis=1).astype(BF16)


def _dsa_in(h, mix_g, positions, w_in):
    t = h.shape[0]
    tm = min(TM_MIX, t)
    spec = lambda w: pl.BlockSpec((tm, w), lambda i: (i, 0))
    widths = (N_HEADS * HEAD_DIM, KV_DIM, KV_DIM, IDX_HEADS * LANES, LANES)
    return pl.pallas_call(
        _dsa_in_kernel,
        grid=(t // tm,),
        in_specs=[spec(D_MODEL), _const_spec((1, D_MODEL)), spec(1), _const_spec((D_MODEL, _P_END)),
                  _const_spec((1, LANES)), _const_spec((1, LANES))],
        out_specs=[spec(w) for w in widths] + [spec(LANES)],
        out_shape=[jax.ShapeDtypeStruct((t, w), BF16) for w in widths]
        + [jax.ShapeDtypeStruct((t, LANES), F32)],
        scratch_shapes=[pltpu.VMEM((tm, D_MODEL), BF16)],
        compiler_params=_cparams("parallel"),
        name="dsa_in_proj",
    )(h, mix_g.reshape(1, -1), positions.reshape(t, 1), _dsa_in_weights(w_in),
      _inv_freq_lanes(ROT_DIM), _inv_freq_lanes(IDX_ROT_DIM))


_INT_MIN = -2 ** 31
_U_NEG_INF = 0x007FFFFF


def _key_to_f32(u):
    bits = jnp.where(u < 0, u ^ _INT_MIN, ~u)
    return lax.bitcast_convert_type(bits, F32)


def _dsa_core_kernel(q_ref, qi_ref, wi_ref, k_ref, v_ref, ki_ref, o_ref,
                     score_s, sel_s, bias_s, lg_s, mx_s, l_s, acc_s, *, k_top):
    qb = q_ref.shape[0]
    sk = k_ref.shape[0]
    n_fold = K_TILE // LANES
    q_end = (pl.program_id(1) + 1) * qb
    n_kt = (q_end + (K_TILE - 1)) // K_TILE
    neg_inf = jnp.float32(-jnp.inf)

    def tile_start(kt):
        return pl.multiple_of(kt * K_TILE, K_TILE)

    def for_tiles_paired(tile_fn):
        def pair(i, c):
            tile_fn(2 * i, c)
            return tile_fn(2 * i + 1, c)

        lax.fori_loop(0, lax.shift_right_logical(n_kt, 1), pair, 0)

        @pl.when(lax.bitwise_and(n_kt, 1) == 1)
        def _():
            tile_fn(n_kt - 1, 0)

    fold_rows = 64

    def key_fold(x, op):
        return op(x.reshape(K_TILE // fold_rows, fold_rows, qb), axis=0)

    def kq_pos(k0):
        shape = (K_TILE, qb)
        return (k0 + lax.broadcasted_iota(I32, shape, 0),
                q_end - qb + lax.broadcasted_iota(I32, shape, 1))

    qi_all = jnp.concatenate([qi_ref[:, hd * LANES:(hd + 1) * LANES] for hd in range(IDX_HEADS)], axis=0)
    wi_t = wi_ref[...].T
    wrows = [wi_t[hd:hd + 1, :] for hd in range(IDX_HEADS)]

    def score_tile(kt, c):
        k0 = tile_start(kt)
        rel = _dot_nt(ki_ref[pl.ds(k0, K_TILE), :], qi_all)
        acc = jnp.zeros((K_TILE, qb), F32)
        for hd in range(IDX_HEADS):
            acc = acc + wrows[hd] * jnp.maximum(rel[:, hd * qb:(hd + 1) * qb], 0.0)
        kpos, qpos = kq_pos(k0)
        score_s[pl.ds(k0, K_TILE), :] = jnp.where(kpos <= qpos, acc, neg_inf)
        return c

    for_tiles_paired(score_tile)

    def count_ge(thr):
        def body(kt, acc):
            tile = score_s[pl.ds(tile_start(kt), K_TILE), :]
            return acc + key_fold(jnp.where(tile >= thr, 1.0, 0.0), jnp.sum)

        acc = lax.fori_loop(0, n_kt, body, jnp.zeros((fold_rows, qb), F32))
        return jnp.sum(acc, axis=0, keepdims=True)

    kf = jnp.float32(k_top)

    def bisect(p, ans):
        bit = lax.shift_left(jnp.int32(1), 31 - p)
        cand = ans | bit
        force = (cand >= 0) & (cand <= _U_NEG_INF)
        ok = (count_ge(_key_to_f32(cand)) >= kf) | force
        return jnp.where(ok, cand, ans)

    ans = lax.fori_loop(0, 32, bisect, jnp.zeros((1, qb), I32))
    thr = _key_to_f32(jnp.where(ans == _U_NEG_INF, ans + 1, ans))
    thr_up = _key_to_f32(ans + 1)

    def count_bin(kt, accs):
        tile = score_s[pl.ds(tile_start(kt), K_TILE), :]
        return (accs[0] + key_fold(jnp.where(tile >= thr_up, 1.0, 0.0), jnp.sum),
                accs[1] + key_fold(jnp.where(tile >= thr, 1.0, 0.0), jnp.sum))

    zero = jnp.zeros((fold_rows, qb), F32)
    above, at_least = lax.fori_loop(0, n_kt, count_bin, (zero, zero))
    c_hi = jnp.sum(above, axis=0, keepdims=True)
    need_bin = kf - c_hi
    in_bin = jnp.sum(at_least, axis=0, keepdims=True) - c_hi
    fits = in_bin <= need_bin
    thr_take = jnp.where(fits, thr, thr_up)
    need0 = jnp.where(fits, 0.0, need_bin)
    rem0 = jnp.where(fits, 0.0, in_bin)

    def init_sel(kt, c):
        k0 = tile_start(kt)
        sel_s[pl.ds(k0, K_TILE), :] = jnp.where(score_s[pl.ds(k0, K_TILE), :] >= thr_take, 0.0, NEG_BIG)
        return c

    lax.fori_loop(0, n_kt, init_sel, 0)

    def active_of(need, rem):
        return (need > 0.0) & (rem > 0.0)

    def bin_cond(carry):
        return carry[2] > 0.0

    def bin_body(carry):
        need, rem, _ = carry
        active = active_of(need, rem)

        def bin_tile(k0):
            s = score_s[pl.ds(k0, K_TILE), :]
            return s, (s >= thr) & (sel_s[pl.ds(k0, K_TILE), :] != 0.0)

        def best_of(kt, best):
            s, in_bin = bin_tile(tile_start(kt))
            return jnp.maximum(best, key_fold(jnp.where(in_bin, s, neg_inf), jnp.max))

        best = lax.fori_loop(0, n_kt, best_of, jnp.full((fold_rows, qb), neg_inf, F32))
        m = jnp.max(best, axis=0, keepdims=True)

        def first_of(kt, first):
            k0 = tile_start(kt)
            s, in_bin = bin_tile(k0)
            kpos, _ = kq_pos(k0)
            return jnp.minimum(first, key_fold(jnp.where((s == m) & in_bin, kpos, sk), jnp.min))

        first = lax.fori_loop(0, n_kt, first_of, jnp.full((fold_rows, qb), sk, I32))
        pick = jnp.min(first, axis=0, keepdims=True)

        def mark(kt, c):
            k0 = tile_start(kt)
            kpos, _ = kq_pos(k0)
            sel_s[pl.ds(k0, K_TILE), :] = jnp.where(active & (kpos == pick), 0.0, sel_s[pl.ds(k0, K_TILE), :])
            return c

        lax.fori_loop(0, n_kt, mark, 0)
        step = jnp.where(active, 1.0, 0.0)
        need, rem = need - step, rem - step
        go = jnp.max(jnp.where(active_of(need, rem), 1.0, 0.0))
        return need, rem, go

    go0 = jnp.max(jnp.where(active_of(need0, rem0), 1.0, 0.0))
    lax.while_loop(bin_cond, bin_body, (need0, rem0, go0))

    def to_rows(kt, c):
        k0 = tile_start(kt)
        bias_s[:, pl.ds(k0, K_TILE)] = sel_s[pl.ds(k0, K_TILE), :].T
        return c

    lax.fori_loop(0, n_kt, to_rows, 0)

    scale = HEAD_DIM ** -0.5 * float(np.log2(np.e))

    def q_rows(g):
        return jnp.concatenate(
            [q_ref[:, (g * HEADS_PER_KV + r) * HEAD_DIM:(g * HEADS_PER_KV + r + 1) * HEAD_DIM]
             for r in range(HEADS_PER_KV)], axis=0)

    def fold(x, op):
        out = x[:, 0:LANES]
        for c in range(1, n_fold):
            out = op(out, x[:, c * LANES:(c + 1) * LANES])
        return out

    mx_s[...] = jnp.full(mx_s.shape, NEG_BIG, F32)
    l_s[...] = jnp.zeros(l_s.shape, F32)
    acc_s[...] = jnp.zeros(acc_s.shape, F32)

    def logit_tile(kt, c):
        k0 = tile_start(kt)
        bias = jnp.concatenate([bias_s[:, pl.ds(k0, K_TILE)]] * HEADS_PER_KV, axis=0)
        for g in range(N_KV_HEADS):
            kt_ = k_ref[pl.ds(k0, K_TILE), g * HEAD_DIM:(g + 1) * HEAD_DIM]
            logits = _dot_nt(q_rows(g), kt_) * scale + bias
            lg_s[g, :, pl.ds(k0, K_TILE)] = logits
            mx_s[g] = jnp.maximum(mx_s[g], fold(logits, jnp.maximum))
        return c

    for_tiles_paired(logit_tile)
    row_max = [jnp.max(mx_s[g], axis=-1, keepdims=True) for g in range(N_KV_HEADS)]

    def att_tile(kt, c):
        k0 = tile_start(kt)
        for g in range(N_KV_HEADS):
            vt_ = v_ref[pl.ds(k0, K_TILE), g * HEAD_DIM:(g + 1) * HEAD_DIM]
            pexp = jnp.exp2(lg_s[g, :, pl.ds(k0, K_TILE)] - row_max[g])
            l_s[g] = l_s[g] + fold(pexp, jnp.add)
            acc_s[g] = acc_s[g] + _dot(pexp.astype(BF16), vt_)
        return c

    for_tiles_paired(att_tile)
    for g in range(N_KV_HEADS):
        og = acc_s[g] / jnp.sum(l_s[g], axis=-1, keepdims=True)
        for r in range(HEADS_PER_KV):
            c0 = (g * HEADS_PER_KV + r) * HEAD_DIM
            o_ref[:, c0:c0 + HEAD_DIM] = og[r * qb:(r + 1) * qb, :].astype(BF16)


def _dsa_core(q, k, v, qi, ki, wi, batch, seq):
    qb = Q_BLOCK
    assert qb == LANES and seq % K_TILE == 0
    k_top = min(TOPK_MAX, seq // 4)
    view = lambda a: a.reshape(batch, seq, a.shape[-1])
    q3, k3, v3, qi3, ki3, wi3 = (view(a) for a in (q, k, v, qi, ki, wi))
    qspec = lambda w: pl.BlockSpec((None, qb, w), lambda b, j: (b, j, 0))
    kspec = lambda w: pl.BlockSpec((None, seq, w), lambda b, j: (b, 0, 0))
    rows = HEADS_PER_KV * qb
    out = pl.pallas_call(
        functools.partial(_dsa_core_kernel, k_top=k_top),
        grid=(batch, seq // qb),
        in_specs=[qspec(D_MODEL), qspec(IDX_HEADS * LANES), qspec(LANES),
                  kspec(KV_DIM), kspec(KV_DIM), kspec(LANES)],
        out_specs=qspec(D_MODEL),
        out_shape=jax.ShapeDtypeStruct((batch, seq, D_MODEL), BF16),
        scratch_shapes=[pltpu.VMEM((seq, qb), F32), pltpu.VMEM((seq, qb), F32),
                        pltpu.VMEM((qb, seq), F32),
                        pltpu.VMEM((N_KV_HEADS, rows, seq), F32),
                        pltpu.VMEM((N_KV_HEADS, rows, LANES), F32),
                        pltpu.VMEM((N_KV_HEADS, rows, LANES), F32),
                        pltpu.VMEM((N_KV_HEADS, rows, HEAD_DIM), F32)],
        compiler_params=_cparams("parallel", "arbitrary"),
        name="dsa_core",
    )(q3, qi3, wi3, k3, v3, ki3)
    return out.reshape(batch * seq, D_MODEL)


def _dsa_out_kernel(h_ref, a_ref, wo_ref, ffng_ref, wr_ref, br_ref, h_out_ref, xn_ref, route_ref,
                    cnt_ref):
    h = h_ref[...] + _dot(a_ref[...], wo_ref[...])
    h_out_ref[...] = h
    xn = _rms(h, ffng_ref[...])
    xn_ref[...] = xn
    route_ref[...], cnt_ref[...] = _route(xn, wr_ref, br_ref)


def _dsa_out(h, att, w_out, ffn_g, wr, br):
    t = h.shape[0]
    tm = min(TM_MIX, t)
    row_spec = pl.BlockSpec((tm, D_MODEL), lambda i: (i, 0))
    return pl.pallas_call(
        _dsa_out_kernel,
        grid=(t // tm,),
        in_specs=[row_spec, row_spec, _const_spec((D_MODEL, D_MODEL)), _const_spec((1, D_MODEL)),
                  _const_spec((D_MODEL, LANES)), _const_spec((1, LANES))],
        out_specs=[row_spec, row_spec, pl.BlockSpec((tm, LANES), lambda i: (i, 0)),
                   pl.BlockSpec((8, LANES), lambda i: (i, 0))],
        out_shape=[jax.ShapeDtypeStruct((t, D_MODEL), F32), jax.ShapeDtypeStruct((t, D_MODEL), F32),
                   jax.ShapeDtypeStruct((t, LANES), F32), jax.ShapeDtypeStruct((t // tm * 8, LANES), F32)],
        compiler_params=_cparams("parallel"),
        name="dsa_out_proj",
    )(h, att, w_out.astype(BF16), ffn_g.reshape(1, -1), wr, br)


def kernel(x, p, positions, mix_norm, a_w_in, a_v_gain, a_w_s, a_b_s, a_w_out, b_w_in, b_w_out,
           ffn_norm, moe_w_group, moe_b_group, moe_w_expert, moe_b_expert, moe_w1, moe_w3, moe_w2,
           ple_norm, ple_w_gate, ple_w_proj, final_norm):
    batch, seq, d = x.shape
    depth = mix_norm.shape[0]
    t = batch * seq
    h = x.reshape(t, d)
    for i in range(depth):
        j = i // 2
        wr, br = _router_weights(moe_w_group[i], moe_b_group[i], moe_w_expert[i], moe_b_expert[i])
        if i % 2 == 0:
            h, xn, route, tile_counts = _gmlp_layer(h, mix_norm[i], a_w_in[j], a_v_gain[j], a_w_s[j],
                                                    a_b_s[j], a_w_out[j], ffn_norm[i], wr, br)
        else:
            q, k, v, qi, ki, wi = _dsa_in(h, mix_norm[i], positions, b_w_in[j])
            att = _dsa_core(q, k, v, qi, ki, wi, batch, seq)
            h, xn, route, tile_counts = _dsa_out(h, att, b_w_out[j], ffn_norm[i], wr, br)
        bufy, dest = _moe(xn, route, tile_counts, moe_w1, moe_w3, moe_w2, i)
        h = _moe_combine_ple(h, route, bufy, dest, p.reshape(depth, t, -1), i, ple_norm[i],
                             ple_w_gate[i], ple_w_proj[i], final_norm, final=(i == depth - 1))
    return h.reshape(batch, seq, d)
```

```python
import functools

import numpy as np
import jax
import jax.numpy as jnp
from jax import lax
from jax.experimental import pallas as pl
from jax.experimental.pallas import tpu as pltpu

F32 = jnp.float32
BF16 = jnp.bfloat16
I32 = jnp.int32

D_MODEL = 1024
EPS = 1e-6
ROPE_THETA = 500000.0
GM_CHUNK = 128
GM_WIDTH = 2 * D_MODEL
GM_GROUPS = 8
GM_GROUP_DIM = GM_WIDTH // GM_GROUPS
N_HEADS = 8
HEAD_DIM = D_MODEL // N_HEADS
N_KV_HEADS = 2
HEADS_PER_KV = N_HEADS // N_KV_HEADS
KV_DIM = N_KV_HEADS * HEAD_DIM
ROT_DIM = HEAD_DIM // 4
IDX_HEADS = 8
IDX_DIM = 64
IDX_ROT_DIM = IDX_DIM // 4
TOPK_MAX = 256
N_EXPERT_GROUPS = 4
EXPERTS_PER_GROUP = 8
N_EXPERTS = N_EXPERT_GROUPS * EXPERTS_PER_GROUP
D_EXPERT = D_MODEL // 2
PLE_DIM = 256

LANES = 128
V7X_VMEM_LIMIT = 56 * 1024 * 1024

TM_MIX = 512
TM_TOK = 512
TM_RANK = 512
MOE_ROWS = 256
Q_BLOCK = 128
K_TILE = 512
NEG_BIG = -1e30

R_E0, R_E1, R_G0, R_G1 = 0, 1, 2, 3
R_EXP0 = N_EXPERT_GROUPS


def _cparams(*sem):
    return pltpu.CompilerParams(dimension_semantics=sem, vmem_limit_bytes=V7X_VMEM_LIMIT)


def _const_spec(shape):
    nd = len(shape)
    return pl.BlockSpec(shape, lambda *_: (0,) * nd, pipeline_mode=pl.Buffered(1))


def _rms(x, g):
    return x * lax.rsqrt(jnp.mean(x * x, axis=-1, keepdims=True) + EPS) * g


def _gelu(x):
    return jax.nn.gelu(x)


def _dot(a, b):
    return jnp.dot(a, b, preferred_element_type=F32)


def _dot_nt(a, b):
    return lax.dot_general(a, b, (((1,), (1,)), ((), ())), preferred_element_type=F32)


def _split_bf16(a):
    hi = a.astype(BF16)
    lo = (a - hi.astype(F32)).astype(BF16)
    return hi, lo


def _dot_3pass(a, b):
    ah, al = _split_bf16(a)
    bh, bl = _split_bf16(b)
    return _dot(ah, bh) + (_dot(al, bh) + _dot(ah, bl))


def _route(xn, wr_ref, br_ref):
    tm = xn.shape[0]
    logits = _dot_3pass(xn, wr_ref[...]) + br_ref[...]
    lane = lax.broadcasted_iota(I32, (tm, LANES), 1)
    neg_inf = jnp.float32(-jnp.inf)

    gmask = lane < N_EXPERT_GROUPS
    gl = jnp.where(gmask, logits, neg_inf)
    gmax = jnp.max(gl, axis=-1, keepdims=True)
    g_sel = jnp.min(jnp.where(gl == gmax, lane, LANES), axis=-1, keepdims=True)
    p_group = 1.0 / jnp.sum(jnp.where(gmask, jnp.exp(gl - gmax), 0.0), axis=-1, keepdims=True)

    lo = R_EXP0 + EXPERTS_PER_GROUP * g_sel
    lmask = (lane >= lo) & (lane < lo + EXPERTS_PER_GROUP)
    ll = jnp.where(lmask, logits, neg_inf)
    lmax = jnp.max(ll, axis=-1, keepdims=True)
    le = jnp.where(lmask, jnp.exp(ll - lmax), 0.0)
    probs = jnp.where(lmask, le / jnp.sum(le, axis=-1, keepdims=True), -1.0)
    p1 = jnp.max(probs, axis=-1, keepdims=True)
    i1 = jnp.min(jnp.where(probs == p1, lane, LANES), axis=-1, keepdims=True)
    probs2 = jnp.where(lane == i1, -1.0, probs)
    p2 = jnp.max(probs2, axis=-1, keepdims=True)
    i2 = jnp.min(jnp.where(probs2 == p2, lane, LANES), axis=-1, keepdims=True)
    denom = p1 + p2
    g1 = p_group * p1 / denom
    g2 = p_group * p2 / denom
    e1 = (i1 - R_EXP0).astype(F32)
    e2 = (i2 - R_EXP0).astype(F32)
    out = jnp.where(lane == R_E0, e1, 0.0)
    out = jnp.where(lane == R_E1, e2, out)
    out = jnp.where(lane == R_G0, g1, out)
    out = jnp.where(lane == R_G1, g2, out)
    hits = jnp.where((lane == i1 - R_EXP0) | (lane == i2 - R_EXP0), 1.0, 0.0)
    counts = jnp.where(lax.broadcasted_iota(I32, (8, LANES), 0) == 0,
                       jnp.sum(hits, axis=0, keepdims=True), 0.0)
    return out, counts


def _router_weights(w_group, b_group, w_expert, b_expert):
    wr = jnp.zeros((D_MODEL, LANES), F32)
    wr = wr.at[:, :N_EXPERT_GROUPS].set(w_group).at[:, R_EXP0:R_EXP0 + N_EXPERTS].set(w_expert)
    br = jnp.zeros((1, LANES), F32)
    br = br.at[0, :N_EXPERT_GROUPS].set(b_group).at[0, R_EXP0:R_EXP0 + N_EXPERTS].set(b_expert)
    return wr, br


def _gmlp_kernel(x_ref, mixg_ref, win_ref, vgain_ref, ws_ref, bst_ref, wout_ref, ffng_ref, wr_ref,
                 br_ref, h_ref, xn_ref, route_ref, cnt_ref, xn_s, v_s, prod_s):
    tm = x_ref.shape[0]
    xn_s[...] = _rms(x_ref[...], mixg_ref[...]).astype(BF16)

    nt = 512
    ssq = jnp.zeros((tm, 1), F32)
    for j in range(GM_WIDTH // nt):
        z = _gelu(_dot(xn_s[...], win_ref[:, GM_WIDTH + j * nt:GM_WIDTH + (j + 1) * nt]))
        v_s[:, j * nt:(j + 1) * nt] = z
        ssq = ssq + jnp.sum(z * z, axis=-1, keepdims=True)
    rinv = lax.rsqrt(ssq * (1.0 / GM_WIDTH) + EPS)

    row = lax.broadcasted_iota(I32, (GM_CHUNK, GM_CHUNK), 0)
    col = lax.broadcasted_iota(I32, (GM_CHUNK, GM_CHUNK), 1)
    tril = row >= col
    gd = GM_GROUP_DIM
    for g in range(GM_GROUPS):
        wm = jnp.where(tril, ws_ref[g], 0.0).astype(BF16)
        vg = (v_s[:, g * gd:(g + 1) * gd] * rinv * vgain_ref[:, g * gd:(g + 1) * gd]).astype(BF16)
        u = _gelu(_dot(xn_s[...], win_ref[:, g * gd:(g + 1) * gd]))
        bias = bst_ref[:, g:g + 1]
        for c in range(tm // GM_CHUNK):
            rows = slice(c * GM_CHUNK, (c + 1) * GM_CHUNK)
            sv = _dot(wm, vg[rows, :]) + bias
            prod_s[rows, g * gd:(g + 1) * gd] = (u[rows, :] * sv).astype(BF16)

    h = x_ref[...] + _dot(prod_s[...], wout_ref[...])
    h_ref[...] = h
    xn = _rms(h, ffng_ref[...])
    xn_ref[...] = xn
    route_ref[...], cnt_ref[...] = _route(xn, wr_ref, br_ref)


def _gmlp_layer(x, mix_g, w_in, v_gain, w_s, b_s, w_out, ffn_g, wr, br):
    t = x.shape[0]
    tm = TM_MIX
    row_spec = pl.BlockSpec((tm, D_MODEL), lambda i: (i, 0))
    return pl.pallas_call(
        _gmlp_kernel,
        grid=(t // tm,),
        in_specs=[
            row_spec,
            _const_spec((1, D_MODEL)),
            _const_spec((D_MODEL, 2 * GM_WIDTH)),
            _const_spec((1, GM_WIDTH)),
            _const_spec((GM_GROUPS, GM_CHUNK, GM_CHUNK)),
            _const_spec((GM_CHUNK, GM_GROUPS)),
            _const_spec((GM_WIDTH, D_MODEL)),
            _const_spec((1, D_MODEL)),
            _const_spec((D_MODEL, LANES)),
            _const_spec((1, LANES)),
        ],
        out_specs=[row_spec, row_spec, pl.BlockSpec((tm, LANES), lambda i: (i, 0)),
                   pl.BlockSpec((8, LANES), lambda i: (i, 0))],
        out_shape=[jax.ShapeDtypeStruct((t, D_MODEL), F32), jax.ShapeDtypeStruct((t, D_MODEL), F32),
                   jax.ShapeDtypeStruct((t, LANES), F32), jax.ShapeDtypeStruct((t // tm * 8, LANES), F32)],
        scratch_shapes=[pltpu.VMEM((tm, D_MODEL), BF16), pltpu.VMEM((tm, GM_WIDTH), F32),
                        pltpu.VMEM((tm, GM_WIDTH), BF16)],
        compiler_params=_cparams("parallel"),
        name="gmlp_mixer",
    )(x, mix_g.reshape(1, -1), w_in.astype(BF16), v_gain.reshape(1, -1), w_s, b_s.T,
      w_out.astype(BF16), ffn_g.reshape(1, -1), wr, br)


def _rank_kernel(route_ref, start_ref, dest_ref, carry_s):
    tm = route_ref.shape[0]

    @pl.when(pl.program_id(0) == 0)
    def _():
        carry_s[...] = jnp.zeros_like(carry_s)

    r = route_ref[...]
    lane = lax.broadcasted_iota(I32, (tm, LANES), 1)
    lanef = lane.astype(F32)
    oh0 = (lanef == r[:, R_E0:R_E0 + 1]).astype(F32)
    oh1 = (lanef == r[:, R_E1:R_E1 + 1]).astype(F32)
    both = oh0 + oh1
    row = lax.broadcasted_iota(I32, (tm, tm), 0)
    col = lax.broadcasted_iota(I32, (tm, tm), 1)
    strict_lower = (row > col).astype(BF16)
    base = start_ref[...] + carry_s[0:1, :] + _dot(strict_lower, both.astype(BF16))
    d0 = jnp.sum(oh0 * base, axis=-1, keepdims=True)
    d1 = jnp.sum(oh1 * (base + oh0), axis=-1, keepdims=True)
    cols = jnp.where(lane == 0, d0, jnp.where(lane == 1, d1, 0.0))
    dest_ref[...] = cols.T[0:8, :].astype(I32)
    carry_s[0:1, :] = carry_s[0:1, :] + jnp.sum(both, axis=0, keepdims=True)


def _moe_rank(route, start):
    t = route.shape[0]
    tm = min(TM_RANK, t)
    nt = t // tm
    dest = pl.pallas_call(
        _rank_kernel,
        grid=(nt,),
        in_specs=[pl.BlockSpec((tm, LANES), lambda i: (i, 0)), _const_spec((1, LANES))],
        out_specs=pl.BlockSpec((8, tm), lambda i: (i, 0)),
        out_shape=jax.ShapeDtypeStruct((nt * 8, tm), I32),
        scratch_shapes=[pltpu.VMEM((8, LANES), F32)],
        compiler_params=_cparams("arbitrary"),
        name="moe_rank",
    )(route, start)
    return dest.reshape(nt, 8, tm)[:, :2, :]


def _row_copy(src_ref, dst_ref, sem, s, d):
    return pltpu.make_async_copy(src_ref.at[pl.ds(s, 1)], dst_ref.at[pl.ds(d, 1)], sem)


_DISPATCH_BUFS = 3


def _dispatch_kernel(dest_ref, xn_ref, bufx_ref, x_s, load_sem, row_sem):
    tm = TM_TOK
    i = pl.program_id(0)
    n = pl.num_programs(0)
    slot = lax.rem(i, _DISPATCH_BUFS)
    nxt = lax.rem(i + 1, _DISPATCH_BUFS)

    def load(tile, s):
        return pltpu.make_async_copy(xn_ref.at[pl.ds(tile * tm, tm)], x_s.at[s], load_sem.at[s])

    def drain_rows(s):
        for _ in range(2 * tm):
            _row_copy(x_s.at[s], bufx_ref, row_sem.at[s], 0, 0).wait()

    @pl.when(i == 0)
    def _():
        load(0, 0).start()

    @pl.when(i >= 2)
    def _():
        drain_rows(nxt)

    @pl.when(i + 1 < n)
    def _():
        load(i + 1, nxt).start()

    load(i, slot).wait()
    for t in range(tm):
        for k in range(2):
            _row_copy(x_s.at[slot], bufx_ref, row_sem.at[slot], t, dest_ref[k * tm + t]).start(priority=k)

    @pl.when(i == n - 1)
    def _():
        drain_rows(slot)

    @pl.when((i == n - 1) & (i >= 1))
    def _():
        drain_rows(lax.rem(i + 2, _DISPATCH_BUFS))


def _moe_dispatch(xn, dest_flat):
    t = xn.shape[0]
    tm = TM_TOK
    return pl.pallas_call(
        _dispatch_kernel,
        grid=(t // tm,),
        in_specs=[pl.BlockSpec((2 * tm,), lambda i: (i,), memory_space=pltpu.SMEM),
                  pl.BlockSpec(memory_space=pl.ANY)],
        out_specs=pl.BlockSpec(memory_space=pl.ANY),
        out_shape=jax.ShapeDtypeStruct((2 * t, D_MODEL), F32),
        scratch_shapes=[pltpu.VMEM((_DISPATCH_BUFS, tm, D_MODEL), F32),
                        pltpu.SemaphoreType.DMA((_DISPATCH_BUFS,)),
                        pltpu.SemaphoreType.DMA((_DISPATCH_BUFS,))],
        compiler_params=_cparams("arbitrary"),
        name="moe_dispatch",
    )(dest_flat, xn)


ROW_SLABS = D_MODEL // LANES


def _expert_kernel(ie_ref, ib_ref, ilo_ref, ihi_ref, ifirst_ref, inext_ref, islot_ref, n_ref,
                   x_ref, w1_ref, w3_ref, w2_ref, y_ref, wf1_s, wf3_s, wf2_s, w1_s, w3_s, w2_s, wsem,
                   *, layer):
    w = pl.program_id(0)
    br = x_ref.shape[0]
    live = w < n_ref[0]
    e = ie_ref[w]
    prev = ie_ref[jnp.maximum(w - 1, 0)]
    slot = islot_ref[w]

    def weight_copies(expert, s):
        return [pltpu.make_async_copy(src.at[layer, expert], dst.at[s], wsem.at[s, i])
                for i, (src, dst) in enumerate(((w1_ref, wf1_s), (w3_ref, wf3_s), (w2_ref, wf2_s)))]

    @pl.when(live & (w == 0))
    def _():
        for c in weight_copies(e, slot):
            c.start()

    @pl.when(live & ((w == 0) | (e != prev)))
    def _():
        for c in weight_copies(e, slot):
            c.wait()

        @pl.when(inext_ref[w] >= 0)
        def _():
            for c in weight_copies(inext_ref[w], 1 - slot):
                c.start()

        w1_s[...] = wf1_s[slot].astype(BF16)
        w3_s[...] = wf3_s[slot].astype(BF16)
        w2_s[...] = wf2_s[slot].astype(BF16)

    def ffn():
        rows = lax.broadcasted_iota(I32, (br, 1), 0)
        mine = (rows >= ilo_ref[w]) & (rows < ihi_ref[w])
        x = jnp.where(mine, x_ref[...], 0.0).astype(BF16)
        a = _dot(x, w1_s[...])
        b = _dot(x, w3_s[...])
        mid = (a * jax.nn.sigmoid(a) * b).astype(BF16)
        return _dot(mid, w2_s[...])

    @pl.when(live & (ifirst_ref[w] == 1))
    def _():
        y = ffn()
        for j in range(ROW_SLABS):
            y_ref[pl.ds(j, br, stride=ROW_SLABS), :] = y[:, j * LANES:(j + 1) * LANES]

    @pl.when(live & (ifirst_ref[w] == 0))
    def _():
        y = ffn()
        for j in range(ROW_SLABS):
            rows_j = pl.ds(j, br, stride=ROW_SLABS)
            y_ref[rows_j, :] = y_ref[rows_j, :] + y[:, j * LANES:(j + 1) * LANES]


def _moe_experts(bufx, items, w1, w3, w2, layer):
    n_rows = bufx.shape[0]
    br = MOE_ROWS
    n_items_max = items[0].shape[0]
    n_prefetch = len(items)

    def row_map(w, *pref):
        return (pref[1][w], 0)

    grid_spec = pltpu.PrefetchScalarGridSpec(
        num_scalar_prefetch=n_prefetch,
        grid=(n_items_max,),
        in_specs=[pl.BlockSpec((br, D_MODEL), row_map),
                  pl.BlockSpec(memory_space=pl.ANY),
                  pl.BlockSpec(memory_space=pl.ANY),
                  pl.BlockSpec(memory_space=pl.ANY)],
        out_specs=pl.BlockSpec((br * ROW_SLABS, LANES), row_map),
        scratch_shapes=[pltpu.VMEM((2, D_MODEL, D_EXPERT), F32), pltpu.VMEM((2, D_MODEL, D_EXPERT), F32),
                        pltpu.VMEM((2, D_EXPERT, D_MODEL), F32),
                        pltpu.VMEM((D_MODEL, D_EXPERT), BF16), pltpu.VMEM((D_MODEL, D_EXPERT), BF16),
                        pltpu.VMEM((D_EXPERT, D_MODEL), BF16),
                        pltpu.SemaphoreType.DMA((2, 3))],
    )
    return pl.pallas_call(
        functools.partial(_expert_kernel, layer=layer),
        grid_spec=grid_spec,
        out_shape=jax.ShapeDtypeStruct((n_rows * ROW_SLABS, LANES), F32),
        compiler_params=_cparams("arbitrary"),
        name="moe_experts",
    )(*items, bufx, w1, w3, w2)


def _moe_plan(counts, t):
    br = MOE_ROWS
    nb = 2 * t // br
    n_items_max = nb + N_EXPERTS - 1
    cnt = counts[:N_EXPERTS].astype(I32)
    end = jnp.cumsum(cnt)
    start = end - cnt

    first_blk = start // br
    n_blk = jnp.where(cnt > 0, (end - 1) // br - first_blk + 1, 0)
    iend = jnp.cumsum(n_blk)
    istart = iend - n_blk
    n_items = iend[-1:]
    w = jnp.minimum(jnp.arange(n_items_max, dtype=I32), n_items[0] - 1)
    ie = jnp.minimum(jnp.sum((iend[None, :] <= w[:, None]).astype(I32), axis=1), N_EXPERTS - 1)
    ib = first_blk[ie] + (w - istart[ie])
    ilo = jnp.maximum(start[ie], ib * br) - ib * br
    ihi = jnp.minimum(end[ie], (ib + 1) * br) - ib * br
    ifirst = jnp.concatenate([jnp.ones((1,), I32), (ib[1:] != ib[:-1]).astype(I32)])
    ids = jnp.arange(N_EXPERTS, dtype=I32)
    used = cnt > 0
    later = (ids[None, :] > ids[:, None]) & used[None, :]
    nxt = jnp.min(jnp.where(later, ids[None, :], N_EXPERTS), axis=1)
    nxt = jnp.where(nxt == N_EXPERTS, -1, nxt)
    slot = (jnp.cumsum(used.astype(I32)) - used.astype(I32)) % 2
    items = tuple(a.astype(I32) for a in (ie, ib, ilo, ihi, ifirst, nxt[ie], slot[ie], n_items))
    return items


def _moe(xn, route, tile_counts, w1, w3, w2, layer):
    t = xn.shape[0]
    counts = jnp.sum(tile_counts.reshape(-1, 8, LANES)[:, 0, :], axis=0)
    start = (jnp.cumsum(counts) - counts).reshape(1, LANES)
    dest = _moe_rank(route, start)
    nt_r, _, tm_r = dest.shape
    dest = dest.reshape(nt_r, 2, tm_r // TM_TOK, TM_TOK).transpose(0, 2, 1, 3).reshape(-1)
    items = _moe_plan(counts, t)
    bufx = _moe_dispatch(xn, dest)
    bufy = _moe_experts(bufx, items, w1, w3, w2, layer)
    return bufy, dest


def _ple_kernel(dest_ref, dest_next_ref, h_ref, route_ref, p_ref, bufy_ref, pleg_ref, wg_ref, wp_ref,
                fing_ref, out_ref, y_s, sem, *, final):
    tm = h_ref.shape[0]
    i = pl.program_id(0)
    last = pl.num_programs(0) - 1
    slot = lax.rem(i, 2)
    rs = ROW_SLABS

    def y_base(s, k):
        base = (s * 2 + k) * (tm * rs)
        return base if isinstance(base, int) else pl.multiple_of(base, tm * rs)

    def row_tile_copy(src_row, s, k, t):
        src = bufy_ref.at[pl.ds(pl.multiple_of(src_row * rs, rs), rs)]
        return pltpu.make_async_copy(src, y_s.at[pl.ds(y_base(s, k) + t * rs, rs)], sem.at[s])

    def gather(dref, s):
        for t in range(tm):
            for k in range(2):
                row_tile_copy(dref[k * tm + t], s, k, t).start(priority=k)

    def drain(s):
        for _ in range(2 * tm):
            row_tile_copy(0, s, 0, 0).wait()

    @pl.when(i == 0)
    def _():
        gather(dest_ref, 0)

    drain(slot)
    gather(dest_next_ref, 1 - slot)

    r = route_ref[...]
    g0, g1 = r[:, R_G0:R_G0 + 1], r[:, R_G1:R_G1 + 1]

    def slab(k, j):
        return y_s[pl.ds(y_base(slot, k) + j, tm, stride=rs), :]

    moe = jnp.concatenate([slab(0, j) * g0 + slab(1, j) * g1 for j in range(rs)], axis=1)
    h = h_ref[...] + moe
    gate = jax.nn.sigmoid(_dot(_rms(h, pleg_ref[...]).astype(BF16), wg_ref[...]))
    h = h + gate * _dot(p_ref[...].astype(BF16), wp_ref[...])
    out_ref[...] = _rms(h, fing_ref[...]) if final else h

    @pl.when(i == last)
    def _():
        drain(1 - slot)


def _moe_combine_ple(h, route, bufy, dest, p_all, layer, ple_g, w_gate, w_proj, final_g, final):
    t = h.shape[0]
    tm = TM_TOK
    nt = t // tm
    row_spec = pl.BlockSpec((tm, D_MODEL), lambda i: (i, 0))
    return pl.pallas_call(
        functools.partial(_ple_kernel, final=final),
        grid=(nt,),
        in_specs=[pl.BlockSpec((2 * tm,), lambda i: (i,), memory_space=pltpu.SMEM),
                  pl.BlockSpec((2 * tm,), lambda i: (jnp.minimum(i + 1, nt - 1),), memory_space=pltpu.SMEM),
                  row_spec,
                  pl.BlockSpec((tm, LANES), lambda i: (i, 0)),
                  pl.BlockSpec((None, tm, PLE_DIM), lambda i: (layer, i, 0)),
                  pl.BlockSpec(memory_space=pl.ANY),
                  _const_spec((1, D_MODEL)),
                  _const_spec((D_MODEL, D_MODEL)),
                  _const_spec((PLE_DIM, D_MODEL)),
                  _const_spec((1, D_MODEL))],
        out_specs=row_spec,
        out_shape=jax.ShapeDtypeStruct((t, D_MODEL), F32),
        scratch_shapes=[pltpu.VMEM((2 * 2 * tm * ROW_SLABS, LANES), F32), pltpu.SemaphoreType.DMA((2,))],
        compiler_params=_cparams("arbitrary"),
        name="moe_combine_ple",
    )(dest, dest, h, route, p_all, bufy, ple_g.reshape(1, -1), w_gate.astype(BF16), w_proj.astype(BF16),
      final_g.reshape(1, -1))


_P_Q = 0
_P_K = _P_Q + N_HEADS * HEAD_DIM
_P_V = _P_K + KV_DIM
_P_QI = _P_V + KV_DIM
_P_KI = _P_QI + IDX_HEADS * LANES
_P_WI = _P_KI + LANES
_P_END = _P_WI + LANES


def _rope_tables(pos, inv_ref, half):
    tm = pos.shape[0]
    lane = lax.broadcasted_iota(I32, (tm, LANES), 1)
    ang = pos * inv_ref[...]
    rot = lane < 2 * half
    cos = jnp.where(rot, jnp.cos(ang), 1.0)
    sin = jnp.sin(ang)
    s_lo = jnp.where(lane < half, -sin, 0.0)
    s_hi = jnp.where(rot & (lane >= half), sin, 0.0)
    return cos, s_lo, s_hi


def _rope(x, tables, half):
    cos, s_lo, s_hi = tables
    return x * cos + pltpu.roll(x, LANES - half, 1) * s_lo + pltpu.roll(x, half, 1) * s_hi


def _dsa_in_kernel(h_ref, g_ref, pos_ref, w_ref, invq_ref, invi_ref,
                   q_ref, k_ref, v_ref, qi_ref, ki_ref, wi_ref, hn_s):
    hn_s[...] = _rms(h_ref[...], g_ref[...]).astype(BF16)
    pos = pos_ref[...].astype(F32)
    tq = _rope_tables(pos, invq_ref, ROT_DIM // 2)
    ti = _rope_tables(pos, invi_ref, IDX_ROT_DIM // 2)

    def proj(c0, width):
        return _dot(hn_s[...], w_ref[:, c0:c0 + width])

    def rope_blocks(z, out_ref, c_out, tables, half):
        for b in range(z.shape[1] // LANES):
            blk = _rope(z[:, b * LANES:(b + 1) * LANES], tables, half)
            out_ref[:, c_out + b * LANES:c_out + (b + 1) * LANES] = blk.astype(BF16)

    nw = 512
    for c in range(0, N_HEADS * HEAD_DIM, nw):
        rope_blocks(proj(_P_Q + c, nw), q_ref, c, tq, ROT_DIM // 2)
    kv = proj(_P_K, 2 * KV_DIM)
    rope_blocks(kv[:, :KV_DIM], k_ref, 0, tq, ROT_DIM // 2)
    v_ref[...] = kv[:, KV_DIM:].astype(BF16)
    for c in range(0, IDX_HEADS * LANES, nw):
        rope_blocks(proj(_P_QI + c, nw), qi_ref, c, ti, IDX_ROT_DIM // 2)
    kw = proj(_P_KI, 2 * LANES)
    rope_blocks(kw[:, :LANES], ki_ref, 0, ti, IDX_ROT_DIM // 2)
    wi_ref[...] = kw[:, LANES:] * (IDX_HEADS ** -0.5 * IDX_DIM ** -0.5)


def _inv_freq_lanes(rot_dim):
    half = rot_dim // 2
    inv = ROPE_THETA ** (-jnp.arange(0, rot_dim, 2, dtype=F32) / rot_dim)
    lanes = jnp.zeros((LANES,), F32).at[:rot_dim].set(jnp.concatenate([inv, inv]))
    del half
    return lanes.reshape(1, LANES)


def _dsa_in_weights(w_in):
    d = w_in.shape[0]
    c_q, c_k, c_v = N_HEADS * HEAD_DIM, KV_DIM, KV_DIM
    c_qi, c_ki = IDX_HEADS * IDX_DIM, IDX_DIM
    o = np.cumsum([0, c_q, c_k, c_v, c_qi, c_ki, IDX_HEADS])
    wq, wk, wv, wqi, wki, wwi = (w_in[:, o[i]:o[i + 1]] for i in range(6))
    wqi = jnp.pad(wqi.reshape(d, IDX_HEADS, IDX_DIM), ((0, 0), (0, 0), (0, LANES - IDX_DIM)))
    wki = jnp.pad(wki, ((0, 0), (0, LANES - IDX_DIM)))
    wwi = jnp.pad(wwi, ((0, 0), (0, LANES - IDX_HEADS)))
    return jnp.concatenate([wq, wk, wv, wqi.reshape(d, IDX_HEADS * LANES), wki, wwi], axis=1).astype(BF16)


def _dsa_in(h, mix_g, positions, w_in):
    t = h.shape[0]
    tm = min(TM_MIX, t)
    spec = lambda w: pl.BlockSpec((tm, w), lambda i: (i, 0))
    widths = (N_HEADS * HEAD_DIM, KV_DIM, KV_DIM, IDX_HEADS * LANES, LANES)
    return pl.pallas_call(
        _dsa_in_kernel,
        grid=(t // tm,),
        in_specs=[spec(D_MODEL), _const_spec((1, D_MODEL)), spec(1), _const_spec((D_MODEL, _P_END)),
                  _const_spec((1, LANES)), _const_spec((1, LANES))],
        out_specs=[spec(w) for w in widths] + [spec(LANES)],
        out_shape=[jax.ShapeDtypeStruct((t, w), BF16) for w in widths]
        + [jax.ShapeDtypeStruct((t, LANES), F32)],
        scratch_shapes=[pltpu.VMEM((tm, D_MODEL), BF16)],
        compiler_params=_cparams("parallel"),
        name="dsa_in_proj",
    )(h, mix_g.reshape(1, -1), positions.reshape(t, 1), _dsa_in_weights(w_in),
      _inv_freq_lanes(ROT_DIM), _inv_freq_lanes(IDX_ROT_DIM))


_INT_MIN = -2 ** 31
_U_NEG_INF = 0x007FFFFF


def _key_to_f32(u):
    bits = jnp.where(u < 0, u ^ _INT_MIN, ~u)
    return lax.bitcast_convert_type(bits, F32)


def _dsa_core_kernel(q_ref, qi_ref, wi_ref, k_ref, v_ref, ki_ref, o_ref,
                     score_s, sel_s, bias_s, lg_s, mx_s, l_s, acc_s, *, k_top):
    qb = q_ref.shape[0]
    sk = k_ref.shape[0]
    n_fold = K_TILE // LANES
    q_end = (pl.program_id(1) + 1) * qb
    n_kt = (q_end + (K_TILE - 1)) // K_TILE
    neg_inf = jnp.float32(-jnp.inf)

    def tile_start(kt):
        return pl.multiple_of(kt * K_TILE, K_TILE)

    def for_tiles_grouped(tile_fn):
        def quad(i, c):
            for u in range(4):
                tile_fn(4 * i + u, c)
            return c

        n_quads = lax.shift_right_logical(n_kt, 2)
        lax.fori_loop(0, n_quads, quad, 0)
        rest = lax.bitwise_and(n_kt, 3)

        @pl.when(rest >= 2)
        def _():
            tile_fn(4 * n_quads, 0)
            tile_fn(4 * n_quads + 1, 0)

        @pl.when(lax.bitwise_and(rest, 1) == 1)
        def _():
            tile_fn(n_kt - 1, 0)

    fold_rows = 64

    def key_fold(x, op):
        return op(x.reshape(K_TILE // fold_rows, fold_rows, qb), axis=0)

    def kq_pos(k0):
        shape = (K_TILE, qb)
        return (k0 + lax.broadcasted_iota(I32, shape, 0),
                q_end - qb + lax.broadcasted_iota(I32, shape, 1))

    qi_all = jnp.concatenate([qi_ref[:, hd * LANES:(hd + 1) * LANES] for hd in range(IDX_HEADS)], axis=0)
    wi_t = wi_ref[...].T
    wrows = [wi_t[hd:hd + 1, :] for hd in range(IDX_HEADS)]

    def score_tile(kt, c):
        k0 = tile_start(kt)
        rel = _dot_nt(ki_ref[pl.ds(k0, K_TILE), :], qi_all)
        acc = jnp.zeros((K_TILE, qb), F32)
        for hd in range(IDX_HEADS):
            acc = acc + wrows[hd] * jnp.maximum(rel[:, hd * qb:(hd + 1) * qb], 0.0)
        kpos, qpos = kq_pos(k0)
        score_s[pl.ds(k0, K_TILE), :] = jnp.where(kpos <= qpos, acc, neg_inf)
        return c

    for_tiles_grouped(score_tile)

    def count_ge(thr):
        def body(kt, acc):
            tile = score_s[pl.ds(tile_start(kt), K_TILE), :]
            return acc + key_fold(jnp.where(tile >= thr, 1.0, 0.0), jnp.sum)

        acc = lax.fori_loop(0, n_kt, body, jnp.zeros((fold_rows, qb), F32))
        return jnp.sum(acc, axis=0, keepdims=True)

    kf = jnp.float32(k_top)

    def bisect(p, ans):
        bit = lax.shift_left(jnp.int32(1), 31 - p)
        cand = ans | bit
        force = (cand >= 0) & (cand <= _U_NEG_INF)
        ok = (count_ge(_key_to_f32(cand)) >= kf) | force
        return jnp.where(ok, cand, ans)

    ans = lax.fori_loop(0, 32, bisect, jnp.zeros((1, qb), I32))
    thr = _key_to_f32(jnp.where(ans == _U_NEG_INF, ans + 1, ans))
    thr_up = _key_to_f32(ans + 1)

    def count_bin(kt, accs):
        tile = score_s[pl.ds(tile_start(kt), K_TILE), :]
        return (accs[0] + key_fold(jnp.where(tile >= thr_up, 1.0, 0.0), jnp.sum),
                accs[1] + key_fold(jnp.where(tile >= thr, 1.0, 0.0), jnp.sum))

    zero = jnp.zeros((fold_rows, qb), F32)
    above, at_least = lax.fori_loop(0, n_kt, count_bin, (zero, zero))
    c_hi = jnp.sum(above, axis=0, keepdims=True)
    need_bin = kf - c_hi
    in_bin = jnp.sum(at_least, axis=0, keepdims=True) - c_hi
    fits = in_bin <= need_bin
    thr_take = jnp.where(fits, thr, thr_up)
    need0 = jnp.where(fits, 0.0, need_bin)
    rem0 = jnp.where(fits, 0.0, in_bin)

    def init_sel(kt, c):
        k0 = tile_start(kt)
        sel_s[pl.ds(k0, K_TILE), :] = jnp.where(score_s[pl.ds(k0, K_TILE), :] >= thr_take, 0.0, NEG_BIG)
        return c

    lax.fori_loop(0, n_kt, init_sel, 0)

    def active_of(need, rem):
        return (need > 0.0) & (rem > 0.0)

    def bin_cond(carry):
        return carry[2] > 0.0

    def bin_body(carry):
        need, rem, _ = carry
        active = active_of(need, rem)

        def bin_tile(k0):
            s = score_s[pl.ds(k0, K_TILE), :]
            return s, (s >= thr) & (sel_s[pl.ds(k0, K_TILE), :] != 0.0)

        def best_of(kt, best):
            s, in_bin = bin_tile(tile_start(kt))
            return jnp.maximum(best, key_fold(jnp.where(in_bin, s, neg_inf), jnp.max))

        best = lax.fori_loop(0, n_kt, best_of, jnp.full((fold_rows, qb), neg_inf, F32))
        m = jnp.max(best, axis=0, keepdims=True)

        def first_of(kt, first):
            k0 = tile_start(kt)
            s, in_bin = bin_tile(k0)
            kpos, _ = kq_pos(k0)
            return jnp.minimum(first, key_fold(jnp.where((s == m) & in_bin, kpos, sk), jnp.min))

        first = lax.fori_loop(0, n_kt, first_of, jnp.full((fold_rows, qb), sk, I32))
        pick = jnp.min(first, axis=0, keepdims=True)

        def mark(kt, c):
            k0 = tile_start(kt)
            kpos, _ = kq_pos(k0)
            sel_s[pl.ds(k0, K_TILE), :] = jnp.where(active & (kpos == pick), 0.0, sel_s[pl.ds(k0, K_TILE), :])
            return c

        lax.fori_loop(0, n_kt, mark, 0)
        step = jnp.where(active, 1.0, 0.0)
        need, rem = need - step, rem - step
        go = jnp.max(jnp.where(active_of(need, rem), 1.0, 0.0))
        return need, rem, go

    go0 = jnp.max(jnp.where(active_of(need0, rem0), 1.0, 0.0))
    lax.while_loop(bin_cond, bin_body, (need0, rem0, go0))

    def to_rows(kt, c):
        k0 = tile_start(kt)
        bias_s[:, pl.ds(k0, K_TILE)] = sel_s[pl.ds(k0, K_TILE), :].T
        return c

    lax.fori_loop(0, n_kt, to_rows, 0)

    scale = HEAD_DIM ** -0.5 * float(np.log2(np.e))

    def q_rows(g):
        return jnp.concatenate(
            [q_ref[:, (g * HEADS_PER_KV + r) * HEAD_DIM:(g * HEADS_PER_KV + r + 1) * HEAD_DIM]
             for r in range(HEADS_PER_KV)], axis=0)

    def fold(x, op):
        out = x[:, 0:LANES]
        for c in range(1, n_fold):
            out = op(out, x[:, c * LANES:(c + 1) * LANES])
        return out

    mx_s[...] = jnp.full(mx_s.shape, NEG_BIG, F32)
    l_s[...] = jnp.zeros(l_s.shape, F32)
    acc_s[...] = jnp.zeros(acc_s.shape, F32)

    def logit_tile(kt, c):
        k0 = tile_start(kt)
        bias = jnp.concatenate([bias_s[:, pl.ds(k0, K_TILE)]] * HEADS_PER_KV, axis=0)
        for g in range(N_KV_HEADS):
            kt_ = k_ref[pl.ds(k0, K_TILE), g * HEAD_DIM:(g + 1) * HEAD_DIM]
            logits = _dot_nt(q_rows(g), kt_) * scale + bias
            lg_s[g, :, pl.ds(k0, K_TILE)] = logits
            mx_s[g] = jnp.maximum(mx_s[g], fold(logits, jnp.maximum))
        return c

    for_tiles_grouped(logit_tile)
    row_max = [jnp.max(mx_s[g], axis=-1, keepdims=True) for g in range(N_KV_HEADS)]

    def att_tile(kt, c):
        k0 = tile_start(kt)
        for g in range(N_KV_HEADS):
            vt_ = v_ref[pl.ds(k0, K_TILE), g * HEAD_DIM:(g + 1) * HEAD_DIM]
            pexp = jnp.exp2(lg_s[g, :, pl.ds(k0, K_TILE)] - row_max[g])
            l_s[g] = l_s[g] + fold(pexp, jnp.add)
            acc_s[g] = acc_s[g] + _dot(pexp.astype(BF16), vt_)
        return c

    for_tiles_grouped(att_tile)
    for g in range(N_KV_HEADS):
        og = acc_s[g] / jnp.sum(l_s[g], axis=-1, keepdims=True)
        for r in range(HEADS_PER_KV):
            c0 = (g * HEADS_PER_KV + r) * HEAD_DIM
            o_ref[:, c0:c0 + HEAD_DIM] = og[r * qb:(r + 1) * qb, :].astype(BF16)


def _dsa_core(q, k, v, qi, ki, wi, batch, seq):
    qb = Q_BLOCK
    assert qb == LANES and seq % K_TILE == 0
    k_top = min(TOPK_MAX, seq // 4)
    view = lambda a: a.reshape(batch, seq, a.shape[-1])
    q3, k3, v3, qi3, ki3, wi3 = (view(a) for a in (q, k, v, qi, ki, wi))
    qspec = lambda w: pl.BlockSpec((None, qb, w), lambda b, j: (b, j, 0))
    kspec = lambda w: pl.BlockSpec((None, seq, w), lambda b, j: (b, 0, 0))
    rows = HEADS_PER_KV * qb
    out = pl.pallas_call(
        functools.partial(_dsa_core_kernel, k_top=k_top),
        grid=(batch, seq // qb),
        in_specs=[qspec(D_MODEL), qspec(IDX_HEADS * LANES), qspec(LANES),
                  kspec(KV_DIM), kspec(KV_DIM), kspec(LANES)],
        out_specs=qspec(D_MODEL),
        out_shape=jax.ShapeDtypeStruct((batch, seq, D_MODEL), BF16),
        scratch_shapes=[pltpu.VMEM((seq, qb), F32), pltpu.VMEM((seq, qb), F32),
                        pltpu.VMEM((qb, seq), F32),
                        pltpu.VMEM((N_KV_HEADS, rows, seq), F32),
                        pltpu.VMEM((N_KV_HEADS, rows, LANES), F32),
                        pltpu.VMEM((N_KV_HEADS, rows, LANES), F32),
                        pltpu.VMEM((N_KV_HEADS, rows, HEAD_DIM), F32)],
        compiler_params=_cparams("parallel", "arbitrary"),
        name="dsa_core",
    )(q3, qi3, wi3, k3, v3, ki3)
    return out.reshape(batch * seq, D_MODEL)


def _dsa_out_kernel(h_ref, a_ref, wo_ref, ffng_ref, wr_ref, br_ref, h_out_ref, xn_ref, route_ref,
                    cnt_ref):
    h = h_ref[...] + _dot(a_ref[...], wo_ref[...])
    h_out_ref[...] = h
    xn = _rms(h, ffng_ref[...])
    xn_ref[...] = xn
    route_ref[...], cnt_ref[...] = _route(xn, wr_ref, br_ref)


def _dsa_out(h, att, w_out, ffn_g, wr, br):
    t = h.shape[0]
    tm = min(TM_MIX, t)
    row_spec = pl.BlockSpec((tm, D_MODEL), lambda i: (i, 0))
    return pl.pallas_call(
        _dsa_out_kernel,
        grid=(t // tm,),
        in_specs=[row_spec, row_spec, _const_spec((D_MODEL, D_MODEL)), _const_spec((1, D_MODEL)),
                  _const_spec((D_MODEL, LANES)), _const_spec((1, LANES))],
        out_specs=[row_spec, row_spec, pl.BlockSpec((tm, LANES), lambda i: (i, 0)),
                   pl.BlockSpec((8, LANES), lambda i: (i, 0))],
        out_shape=[jax.ShapeDtypeStruct((t, D_MODEL), F32), jax.ShapeDtypeStruct((t, D_MODEL), F32),
                   jax.ShapeDtypeStruct((t, LANES), F32), jax.ShapeDtypeStruct((t // tm * 8, LANES), F32)],
        compiler_params=_cparams("parallel"),
        name="dsa_out_proj",
    )(h, att, w_out.astype(BF16), ffn_g.reshape(1, -1), wr, br)


def kernel(x, p, positions, mix_norm, a_w_in, a_v_gain, a_w_s, a_b_s, a_w_out, b_w_in, b_w_out,
           ffn_norm, moe_w_group, moe_b_group, moe_w_expert, moe_b_expert, moe_w1, moe_w3, moe_w2,
           ple_norm, ple_w_gate, ple_w_proj, final_norm):
    batch, seq, d = x.shape
    depth = mix_norm.shape[0]
    t = batch * seq
    h = x.reshape(t, d)
    for i in range(depth):
        j = i // 2
        wr, br = _router_weights(moe_w_group[i], moe_b_group[i], moe_w_expert[i], moe_b_expert[i])
        if i % 2 == 0:
            h, xn, route, tile_counts = _gmlp_layer(h, mix_norm[i], a_w_in[j], a_v_gain[j], a_w_s[j],
                                                    a_b_s[j], a_w_out[j], ffn_norm[i], wr, br)
        else:
            q, k, v, qi, ki, wi = _dsa_in(h, mix_norm[i], positions, b_w_in[j])
            att = _dsa_core(q, k, v, qi, ki, wi, batch, seq)
            h, xn, route, tile_counts = _dsa_out(h, att, b_w_out[j], ffn_norm[i], wr, br)
        bufy, dest = _moe(xn, route, tile_counts, moe_w1, moe_w3, moe_w2, i)
        h = _moe_combine_ple(h, route, bufy, dest, p.reshape(depth, t, -1), i, ple_norm[i],
                             ple_w_gate[i], ple_w_proj[i], final_norm, final=(i == depth - 1))
    return h.reshape(batch, seq, d)
```

```python
import functools

import numpy as np
import jax
import jax.numpy as jnp
from jax import lax
from jax.experimental import pallas as pl
from jax.experimental.pallas import tpu as pltpu

F32 = jnp.float32
BF16 = jnp.bfloat16
I32 = jnp.int32

D_MODEL = 1024
EPS = 1e-6
ROPE_THETA = 500000.0
GM_CHUNK = 128
GM_WIDTH = 2 * D_MODEL
GM_GROUPS = 8
GM_GROUP_DIM = GM_WIDTH // GM_GROUPS
N_HEADS = 8
HEAD_DIM = D_MODEL // N_HEADS
N_KV_HEADS = 2
HEADS_PER_KV = N_HEADS // N_KV_HEADS
KV_DIM = N_KV_HEADS * HEAD_DIM
ROT_DIM = HEAD_DIM // 4
IDX_HEADS = 8
IDX_DIM = 64
IDX_ROT_DIM = IDX_DIM // 4
TOPK_MAX = 256
N_EXPERT_GROUPS = 4
EXPERTS_PER_GROUP = 8
N_EXPERTS = N_EXPERT_GROUPS * EXPERTS_PER_GROUP
D_EXPERT = D_MODEL // 2
PLE_DIM = 256

LANES = 128
V7X_VMEM_LIMIT = 56 * 1024 * 1024

TM_MIX = 512
TM_DISPATCH = 512
TM_PLE = 256
TM_RANK = 512
MOE_ROWS = 256
Q_BLOCK = 128
K_TILE = 512
NEG_BIG = -1e30

R_E0, R_E1, R_G0, R_G1 = 0, 1, 2, 3
R_EXP0 = N_EXPERT_GROUPS


def _cparams(*sem):
    return pltpu.CompilerParams(dimension_semantics=sem, vmem_limit_bytes=V7X_VMEM_LIMIT)


def _const_spec(shape):
    nd = len(shape)
    return pl.BlockSpec(shape, lambda *_: (0,) * nd, pipeline_mode=pl.Buffered(1))


def _rms(x, g):
    return x * lax.rsqrt(jnp.mean(x * x, axis=-1, keepdims=True) + EPS) * g


def _gelu(x):
    return jax.nn.gelu(x)


def _dot(a, b):
    return jnp.dot(a, b, preferred_element_type=F32)


def _dot_nt(a, b):
    return lax.dot_general(a, b, (((1,), (1,)), ((), ())), preferred_element_type=F32)


def _split_bf16(a):
    hi = a.astype(BF16)
    lo = (a - hi.astype(F32)).astype(BF16)
    return hi, lo


def _dot_3pass(a, b):
    ah, al = _split_bf16(a)
    bh, bl = _split_bf16(b)
    return _dot(ah, bh) + (_dot(al, bh) + _dot(ah, bl))


def _route(xn, wr_ref, br_ref):
    tm = xn.shape[0]
    logits = _dot_3pass(xn, wr_ref[...]) + br_ref[...]
    lane = lax.broadcasted_iota(I32, (tm, LANES), 1)
    neg_inf = jnp.float32(-jnp.inf)

    gmask = lane < N_EXPERT_GROUPS
    gl = jnp.where(gmask, logits, neg_inf)
    gmax = jnp.max(gl, axis=-1, keepdims=True)
    g_sel = jnp.min(jnp.where(gl == gmax, lane, LANES), axis=-1, keepdims=True)
    p_group = 1.0 / jnp.sum(jnp.where(gmask, jnp.exp(gl - gmax), 0.0), axis=-1, keepdims=True)

    lo = R_EXP0 + EXPERTS_PER_GROUP * g_sel
    lmask = (lane >= lo) & (lane < lo + EXPERTS_PER_GROUP)
    ll = jnp.where(lmask, logits, neg_inf)
    lmax = jnp.max(ll, axis=-1, keepdims=True)
    le = jnp.where(lmask, jnp.exp(ll - lmax), 0.0)
    probs = jnp.where(lmask, le / jnp.sum(le, axis=-1, keepdims=True), -1.0)
    p1 = jnp.max(probs, axis=-1, keepdims=True)
    i1 = jnp.min(jnp.where(probs == p1, lane, LANES), axis=-1, keepdims=True)
    probs2 = jnp.where(lane == i1, -1.0, probs)
    p2 = jnp.max(probs2, axis=-1, keepdims=True)
    i2 = jnp.min(jnp.where(probs2 == p2, lane, LANES), axis=-1, keepdims=True)
    denom = p1 + p2
    g1 = p_group * p1 / denom
    g2 = p_group * p2 / denom
    e1 = (i1 - R_EXP0).astype(F32)
    e2 = (i2 - R_EXP0).astype(F32)
    out = jnp.where(lane == R_E0, e1, 0.0)
    out = jnp.where(lane == R_E1, e2, out)
    out = jnp.where(lane == R_G0, g1, out)
    out = jnp.where(lane == R_G1, g2, out)
    hits = jnp.where((lane == i1 - R_EXP0) | (lane == i2 - R_EXP0), 1.0, 0.0)
    counts = jnp.where(lax.broadcasted_iota(I32, (8, LANES), 0) == 0,
                       jnp.sum(hits, axis=0, keepdims=True), 0.0)
    return out, counts


def _router_weights(w_group, b_group, w_expert, b_expert):
    wr = jnp.zeros((D_MODEL, LANES), F32)
    wr = wr.at[:, :N_EXPERT_GROUPS].set(w_group).at[:, R_EXP0:R_EXP0 + N_EXPERTS].set(w_expert)
    br = jnp.zeros((1, LANES), F32)
    br = br.at[0, :N_EXPERT_GROUPS].set(b_group).at[0, R_EXP0:R_EXP0 + N_EXPERTS].set(b_expert)
    return wr, br


def _gmlp_kernel(x_ref, mixg_ref, win_ref, vgain_ref, ws_ref, bst_ref, wout_ref, ffng_ref, wr_ref,
                 br_ref, h_ref, xn_ref, route_ref, cnt_ref, xn_s, v_s, prod_s):
    tm = x_ref.shape[0]
    xn_s[...] = _rms(x_ref[...], mixg_ref[...]).astype(BF16)

    nt = 512
    ssq = jnp.zeros((tm, 1), F32)
    for j in range(GM_WIDTH // nt):
        z = _gelu(_dot(xn_s[...], win_ref[:, GM_WIDTH + j * nt:GM_WIDTH + (j + 1) * nt]))
        v_s[:, j * nt:(j + 1) * nt] = z
        ssq = ssq + jnp.sum(z * z, axis=-1, keepdims=True)
    rinv = lax.rsqrt(ssq * (1.0 / GM_WIDTH) + EPS)

    row = lax.broadcasted_iota(I32, (GM_CHUNK, GM_CHUNK), 0)
    col = lax.broadcasted_iota(I32, (GM_CHUNK, GM_CHUNK), 1)
    tril = row >= col
    gd = GM_GROUP_DIM
    for g in range(GM_GROUPS):
        wm = jnp.where(tril, ws_ref[g], 0.0).astype(BF16)
        vg = (v_s[:, g * gd:(g + 1) * gd] * rinv * vgain_ref[:, g * gd:(g + 1) * gd]).astype(BF16)
        u = _gelu(_dot(xn_s[...], win_ref[:, g * gd:(g + 1) * gd]))
        bias = bst_ref[:, g:g + 1]
        for c in range(tm // GM_CHUNK):
            rows = slice(c * GM_CHUNK, (c + 1) * GM_CHUNK)
            sv = _dot(wm, vg[rows, :]) + bias
            prod_s[rows, g * gd:(g + 1) * gd] = (u[rows, :] * sv).astype(BF16)

    h = x_ref[...] + _dot(prod_s[...], wout_ref[...])
    h_ref[...] = h
    xn = _rms(h, ffng_ref[...])
    xn_ref[...] = xn
    route_ref[...], cnt_ref[...] = _route(xn, wr_ref, br_ref)


def _gmlp_layer(x, mix_g, w_in, v_gain, w_s, b_s, w_out, ffn_g, wr, br):
    t = x.shape[0]
    tm = TM_MIX
    row_spec = pl.BlockSpec((tm, D_MODEL), lambda i: (i, 0))
    return pl.pallas_call(
        _gmlp_kernel,
        grid=(t // tm,),
        in_specs=[
            row_spec,
            _const_spec((1, D_MODEL)),
            _const_spec((D_MODEL, 2 * GM_WIDTH)),
            _const_spec((1, GM_WIDTH)),
            _const_spec((GM_GROUPS, GM_CHUNK, GM_CHUNK)),
            _const_spec((GM_CHUNK, GM_GROUPS)),
            _const_spec((GM_WIDTH, D_MODEL)),
            _const_spec((1, D_MODEL)),
            _const_spec((D_MODEL, LANES)),
            _const_spec((1, LANES)),
        ],
        out_specs=[row_spec, row_spec, pl.BlockSpec((tm, LANES), lambda i: (i, 0)),
                   pl.BlockSpec((8, LANES), lambda i: (i, 0))],
        out_shape=[jax.ShapeDtypeStruct((t, D_MODEL), F32), jax.ShapeDtypeStruct((t, D_MODEL), F32),
                   jax.ShapeDtypeStruct((t, LANES), F32), jax.ShapeDtypeStruct((t // tm * 8, LANES), F32)],
        scratch_shapes=[pltpu.VMEM((tm, D_MODEL), BF16), pltpu.VMEM((tm, GM_WIDTH), F32),
                        pltpu.VMEM((tm, GM_WIDTH), BF16)],
        compiler_params=_cparams("parallel"),
        name="gmlp_mixer",
    )(x, mix_g.reshape(1, -1), w_in.astype(BF16), v_gain.reshape(1, -1), w_s, b_s.T,
      w_out.astype(BF16), ffn_g.reshape(1, -1), wr, br)


def _rank_kernel(route_ref, start_ref, dest_ref, carry_s):
    tm = route_ref.shape[0]

    @pl.when(pl.program_id(0) == 0)
    def _():
        carry_s[...] = jnp.zeros_like(carry_s)

    r = route_ref[...]
    lane = lax.broadcasted_iota(I32, (tm, LANES), 1)
    lanef = lane.astype(F32)
    oh0 = (lanef == r[:, R_E0:R_E0 + 1]).astype(F32)
    oh1 = (lanef == r[:, R_E1:R_E1 + 1]).astype(F32)
    both = oh0 + oh1
    row = lax.broadcasted_iota(I32, (tm, tm), 0)
    col = lax.broadcasted_iota(I32, (tm, tm), 1)
    strict_lower = (row > col).astype(BF16)
    base = start_ref[...] + carry_s[0:1, :] + _dot(strict_lower, both.astype(BF16))
    d0 = jnp.sum(oh0 * base, axis=-1, keepdims=True)
    d1 = jnp.sum(oh1 * (base + oh0), axis=-1, keepdims=True)
    cols = jnp.where(lane == 0, d0, jnp.where(lane == 1, d1, 0.0))
    dest_ref[...] = cols.T[0:8, :].astype(I32)
    carry_s[0:1, :] = carry_s[0:1, :] + jnp.sum(both, axis=0, keepdims=True)


def _moe_rank(route, start):
    t = route.shape[0]
    tm = min(TM_RANK, t)
    nt = t // tm
    dest = pl.pallas_call(
        _rank_kernel,
        grid=(nt,),
        in_specs=[pl.BlockSpec((tm, LANES), lambda i: (i, 0)), _const_spec((1, LANES))],
        out_specs=pl.BlockSpec((8, tm), lambda i: (i, 0)),
        out_shape=jax.ShapeDtypeStruct((nt * 8, tm), I32),
        scratch_shapes=[pltpu.VMEM((8, LANES), F32)],
        compiler_params=_cparams("arbitrary"),
        name="moe_rank",
    )(route, start)
    return dest.reshape(nt, 8, tm)[:, :2, :]


def _row_copy(src_ref, dst_ref, sem, s, d):
    return pltpu.make_async_copy(src_ref.at[pl.ds(s, 1)], dst_ref.at[pl.ds(d, 1)], sem)


_DISPATCH_BUFS = 3


def _dispatch_kernel(dest_ref, xn_ref, bufx_ref, x_s, load_sem, row_sem):
    tm = TM_DISPATCH
    i = pl.program_id(0)
    n = pl.num_programs(0)
    slot = lax.rem(i, _DISPATCH_BUFS)
    nxt = lax.rem(i + 1, _DISPATCH_BUFS)

    def load(tile, s):
        return pltpu.make_async_copy(xn_ref.at[pl.ds(tile * tm, tm)], x_s.at[s], load_sem.at[s])

    def drain_rows(s):
        for _ in range(2 * tm):
            _row_copy(x_s.at[s], bufx_ref, row_sem.at[s], 0, 0).wait()

    @pl.when(i == 0)
    def _():
        load(0, 0).start()

    @pl.when(i >= 2)
    def _():
        drain_rows(nxt)

    @pl.when(i + 1 < n)
    def _():
        load(i + 1, nxt).start()

    load(i, slot).wait()
    for t in range(tm):
        for k in range(2):
            _row_copy(x_s.at[slot], bufx_ref, row_sem.at[slot], t, dest_ref[k * tm + t]).start(priority=k)

    @pl.when(i == n - 1)
    def _():
        drain_rows(slot)

    @pl.when((i == n - 1) & (i >= 1))
    def _():
        drain_rows(lax.rem(i + 2, _DISPATCH_BUFS))


def _moe_dispatch(xn, dest_flat):
    t = xn.shape[0]
    tm = TM_DISPATCH
    return pl.pallas_call(
        _dispatch_kernel,
        grid=(t // tm,),
        in_specs=[pl.BlockSpec((2 * tm,), lambda i: (i,), memory_space=pltpu.SMEM),
                  pl.BlockSpec(memory_space=pl.ANY)],
        out_specs=pl.BlockSpec(memory_space=pl.ANY),
        out_shape=jax.ShapeDtypeStruct((2 * t, D_MODEL), F32),
        scratch_shapes=[pltpu.VMEM((_DISPATCH_BUFS, tm, D_MODEL), F32),
                        pltpu.SemaphoreType.DMA((_DISPATCH_BUFS,)),
                        pltpu.SemaphoreType.DMA((_DISPATCH_BUFS,))],
        compiler_params=_cparams("arbitrary"),
        name="moe_dispatch",
    )(dest_flat, xn)


ROW_SLABS = D_MODEL // LANES


def _expert_kernel(ie_ref, ib_ref, ilo_ref, ihi_ref, ifirst_ref, inext_ref, islot_ref, n_ref,
                   x_ref, w1_ref, w3_ref, w2_ref, y_ref, wf1_s, wf3_s, wf2_s, w1_s, w3_s, w2_s, wsem,
                   *, layer):
    w = pl.program_id(0)
    br = x_ref.shape[0]
    live = w < n_ref[0]
    e = ie_ref[w]
    prev = ie_ref[jnp.maximum(w - 1, 0)]
    slot = islot_ref[w]

    def weight_copies(expert, s):
        return [pltpu.make_async_copy(src.at[layer, expert], dst.at[s], wsem.at[s, i])
                for i, (src, dst) in enumerate(((w1_ref, wf1_s), (w3_ref, wf3_s), (w2_ref, wf2_s)))]

    @pl.when(live & (w == 0))
    def _():
        for c in weight_copies(e, slot):
            c.start()

    @pl.when(live & ((w == 0) | (e != prev)))
    def _():
        for c in weight_copies(e, slot):
            c.wait()

        @pl.when(inext_ref[w] >= 0)
        def _():
            for c in weight_copies(inext_ref[w], 1 - slot):
                c.start()

        w1_s[...] = wf1_s[slot].astype(BF16)
        w3_s[...] = wf3_s[slot].astype(BF16)
        w2_s[...] = wf2_s[slot].astype(BF16)

    def ffn():
        rows = lax.broadcasted_iota(I32, (br, 1), 0)
        mine = (rows >= ilo_ref[w]) & (rows < ihi_ref[w])
        x = jnp.where(mine, x_ref[...], 0.0).astype(BF16)
        a = _dot(x, w1_s[...])
        b = _dot(x, w3_s[...])
        mid = (a * jax.nn.sigmoid(a) * b).astype(BF16)
        return _dot(mid, w2_s[...])

    @pl.when(live & (ifirst_ref[w] == 1))
    def _():
        y = ffn()
        for j in range(ROW_SLABS):
            y_ref[pl.ds(j, br, stride=ROW_SLABS), :] = y[:, j * LANES:(j + 1) * LANES]

    @pl.when(live & (ifirst_ref[w] == 0))
    def _():
        y = ffn()
        for j in range(ROW_SLABS):
            rows_j = pl.ds(j, br, stride=ROW_SLABS)
            y_ref[rows_j, :] = y_ref[rows_j, :] + y[:, j * LANES:(j + 1) * LANES]


def _moe_experts(bufx, items, w1, w3, w2, layer):
    n_rows = bufx.shape[0]
    br = MOE_ROWS
    n_items_max = items[0].shape[0]
    n_prefetch = len(items)

    def row_map(w, *pref):
        return (pref[1][w], 0)

    grid_spec = pltpu.PrefetchScalarGridSpec(
        num_scalar_prefetch=n_prefetch,
        grid=(n_items_max,),
        in_specs=[pl.BlockSpec((br, D_MODEL), row_map),
                  pl.BlockSpec(memory_space=pl.ANY),
                  pl.BlockSpec(memory_space=pl.ANY),
                  pl.BlockSpec(memory_space=pl.ANY)],
        out_specs=pl.BlockSpec((br * ROW_SLABS, LANES), row_map),
        scratch_shapes=[pltpu.VMEM((2, D_MODEL, D_EXPERT), F32), pltpu.VMEM((2, D_MODEL, D_EXPERT), F32),
                        pltpu.VMEM((2, D_EXPERT, D_MODEL), F32),
                        pltpu.VMEM((D_MODEL, D_EXPERT), BF16), pltpu.VMEM((D_MODEL, D_EXPERT), BF16),
                        pltpu.VMEM((D_EXPERT, D_MODEL), BF16),
                        pltpu.SemaphoreType.DMA((2, 3))],
    )
    return pl.pallas_call(
        functools.partial(_expert_kernel, layer=layer),
        grid_spec=grid_spec,
        out_shape=jax.ShapeDtypeStruct((n_rows * ROW_SLABS, LANES), F32),
        compiler_params=_cparams("arbitrary"),
        name="moe_experts",
    )(*items, bufx, w1, w3, w2)


def _moe_plan(counts, t):
    br = MOE_ROWS
    nb = 2 * t // br
    n_items_max = nb + N_EXPERTS - 1
    cnt = counts[:N_EXPERTS].astype(I32)
    end = jnp.cumsum(cnt)
    start = end - cnt

    first_blk = start // br
    n_blk = jnp.where(cnt > 0, (end - 1) // br - first_blk + 1, 0)
    iend = jnp.cumsum(n_blk)
    istart = iend - n_blk
    n_items = iend[-1:]
    w = jnp.minimum(jnp.arange(n_items_max, dtype=I32), n_items[0] - 1)
    ie = jnp.minimum(jnp.sum((iend[None, :] <= w[:, None]).astype(I32), axis=1), N_EXPERTS - 1)
    ib = first_blk[ie] + (w - istart[ie])
    ilo = jnp.maximum(start[ie], ib * br) - ib * br
    ihi = jnp.minimum(end[ie], (ib + 1) * br) - ib * br
    ifirst = jnp.concatenate([jnp.ones((1,), I32), (ib[1:] != ib[:-1]).astype(I32)])
    ids = jnp.arange(N_EXPERTS, dtype=I32)
    used = cnt > 0
    later = (ids[None, :] > ids[:, None]) & used[None, :]
    nxt = jnp.min(jnp.where(later, ids[None, :], N_EXPERTS), axis=1)
    nxt = jnp.where(nxt == N_EXPERTS, -1, nxt)
    slot = (jnp.cumsum(used.astype(I32)) - used.astype(I32)) % 2
    items = tuple(a.astype(I32) for a in (ie, ib, ilo, ihi, ifirst, nxt[ie], slot[ie], n_items))
    return items


def _moe(xn, route, tile_counts, w1, w3, w2, layer):
    t = xn.shape[0]
    counts = jnp.sum(tile_counts.reshape(-1, 8, LANES)[:, 0, :], axis=0)
    start = (jnp.cumsum(counts) - counts).reshape(1, LANES)
    dest = _moe_rank(route, start)
    nt_r, _, tm_r = dest.shape

    def regroup(tm):
        return dest.reshape(nt_r, 2, tm_r // tm, tm).transpose(0, 2, 1, 3).reshape(-1)

    items = _moe_plan(counts, t)
    bufx = _moe_dispatch(xn, regroup(TM_DISPATCH))
    bufy = _moe_experts(bufx, items, w1, w3, w2, layer)
    return bufy, regroup(TM_PLE)


def _ple_kernel(dest_ref, dest_next_ref, h_ref, route_ref, p_ref, bufy_ref, pleg_ref, wg_ref, wp_ref,
                fing_ref, out_ref, y_s, sem, *, final):
    tm = h_ref.shape[0]
    i = pl.program_id(0)
    last = pl.num_programs(0) - 1
    slot = lax.rem(i, 2)
    rs = ROW_SLABS

    def y_base(s, k):
        base = (s * 2 + k) * (tm * rs)
        return base if isinstance(base, int) else pl.multiple_of(base, tm * rs)

    def row_tile_copy(src_row, s, k, t):
        src = bufy_ref.at[pl.ds(pl.multiple_of(src_row * rs, rs), rs)]
        return pltpu.make_async_copy(src, y_s.at[pl.ds(y_base(s, k) + t * rs, rs)], sem.at[s])

    def gather(dref, s):
        for t in range(tm):
            for k in range(2):
                row_tile_copy(dref[k * tm + t], s, k, t).start(priority=k)

    def drain(s):
        for _ in range(2 * tm):
            row_tile_copy(0, s, 0, 0).wait()

    @pl.when(i == 0)
    def _():
        gather(dest_ref, 0)

    drain(slot)
    gather(dest_next_ref, 1 - slot)

    r = route_ref[...]
    g0, g1 = r[:, R_G0:R_G0 + 1], r[:, R_G1:R_G1 + 1]

    def slab(k, j):
        return y_s[pl.ds(y_base(slot, k) + j, tm, stride=rs), :]

    moe = jnp.concatenate([slab(0, j) * g0 + slab(1, j) * g1 for j in range(rs)], axis=1)
    h = h_ref[...] + moe
    gate = jax.nn.sigmoid(_dot(_rms(h, pleg_ref[...]).astype(BF16), wg_ref[...]))
    h = h + gate * _dot(p_ref[...].astype(BF16), wp_ref[...])
    out_ref[...] = _rms(h, fing_ref[...]) if final else h

    @pl.when(i == last)
    def _():
        drain(1 - slot)


def _moe_combine_ple(h, route, bufy, dest, p_all, layer, ple_g, w_gate, w_proj, final_g, final):
    t = h.shape[0]
    tm = TM_PLE
    nt = t // tm
    row_spec = pl.BlockSpec((tm, D_MODEL), lambda i: (i, 0))
    return pl.pallas_call(
        functools.partial(_ple_kernel, final=final),
        grid=(nt,),
        in_specs=[pl.BlockSpec((2 * tm,), lambda i: (i,), memory_space=pltpu.SMEM),
                  pl.BlockSpec((2 * tm,), lambda i: (jnp.minimum(i + 1, nt - 1),), memory_space=pltpu.SMEM),
                  row_spec,
                  pl.BlockSpec((tm, LANES), lambda i: (i, 0)),
                  pl.BlockSpec((None, tm, PLE_DIM), lambda i: (layer, i, 0)),
                  pl.BlockSpec(memory_space=pl.ANY),
                  _const_spec((1, D_MODEL)),
                  _const_spec((D_MODEL, D_MODEL)),
                  _const_spec((PLE_DIM, D_MODEL)),
                  _const_spec((1, D_MODEL))],
        out_specs=row_spec,
        out_shape=jax.ShapeDtypeStruct((t, D_MODEL), F32),
        scratch_shapes=[pltpu.VMEM((2 * 2 * tm * ROW_SLABS, LANES), F32), pltpu.SemaphoreType.DMA((2,))],
        compiler_params=_cparams("arbitrary"),
        name="moe_combine_ple",
    )(dest, dest, h, route, p_all, bufy, ple_g.reshape(1, -1), w_gate.astype(BF16), w_proj.astype(BF16),
      final_g.reshape(1, -1))


_P_Q = 0
_P_K = _P_Q + N_HEADS * HEAD_DIM
_P_V = _P_K + KV_DIM
_P_QI = _P_V + KV_DIM
_P_KI = _P_QI + IDX_HEADS * LANES
_P_WI = _P_KI + LANES
_P_END = _P_WI + LANES


def _rope_tables(pos, inv_ref, half):
    tm = pos.shape[0]
    lane = lax.broadcasted_iota(I32, (tm, LANES), 1)
    ang = pos * inv_ref[...]
    rot = lane < 2 * half
    cos = jnp.where(rot, jnp.cos(ang), 1.0)
    sin = jnp.sin(ang)
    s_lo = jnp.where(lane < half, -sin, 0.0)
    s_hi = jnp.where(rot & (lane >= half), sin, 0.0)
    return cos, s_lo, s_hi


def _rope(x, tables, half):
    cos, s_lo, s_hi = tables
    return x * cos + pltpu.roll(x, LANES - half, 1) * s_lo + pltpu.roll(x, half, 1) * s_hi


def _dsa_in_kernel(h_ref, g_ref, pos_ref, w_ref, invq_ref, invi_ref,
                   q_ref, k_ref, v_ref, qi_ref, ki_ref, wi_ref, hn_s):
    hn_s[...] = _rms(h_ref[...], g_ref[...]).astype(BF16)
    pos = pos_ref[...].astype(F32)
    tq = _rope_tables(pos, invq_ref, ROT_DIM // 2)
    ti = _rope_tables(pos, invi_ref, IDX_ROT_DIM // 2)

    def proj(c0, width):
        return _dot(hn_s[...], w_ref[:, c0:c0 + width])

    def rope_blocks(z, out_ref, c_out, tables, half):
        for b in range(z.shape[1] // LANES):
            blk = _rope(z[:, b * LANES:(b + 1) * LANES], tables, half)
            out_ref[:, c_out + b * LANES:c_out + (b + 1) * LANES] = blk.astype(BF16)

    nw = 512
    for c in range(0, N_HEADS * HEAD_DIM, nw):
        rope_blocks(proj(_P_Q + c, nw), q_ref, c, tq, ROT_DIM // 2)
    kv = proj(_P_K, 2 * KV_DIM)
    rope_blocks(kv[:, :KV_DIM], k_ref, 0, tq, ROT_DIM // 2)
    v_ref[...] = kv[:, KV_DIM:].astype(BF16)
    for c in range(0, IDX_HEADS * LANES, nw):
        rope_blocks(proj(_P_QI + c, nw), qi_ref, c, ti, IDX_ROT_DIM // 2)
    kw = proj(_P_KI, 2 * LANES)
    rope_blocks(kw[:, :LANES], ki_ref, 0, ti, IDX_ROT_DIM // 2)
    wi_ref[...] = kw[:, LANES:] * (IDX_HEADS ** -0.5 * IDX_DIM ** -0.5)


def _inv_freq_lanes(rot_dim):
    half = rot_dim // 2
    inv = ROPE_THETA ** (-jnp.arange(0, rot_dim, 2, dtype=F32) / rot_dim)
    lanes = jnp.zeros((LANES,), F32).at[:rot_dim].set(jnp.concatenate([inv, inv]))
    del half
    return lanes.reshape(1, LANES)


def _dsa_in_weights(w_in):
    d = w_in.shape[0]
    c_q, c_k, c_v = N_HEADS * HEAD_DIM, KV_DIM, KV_DIM
    c_qi, c_ki = IDX_HEADS * IDX_DIM, IDX_DIM
    o = np.cumsum([0, c_q, c_k, c_v, c_qi, c_ki, IDX_HEADS])
    wq, wk, wv, wqi, wki, wwi = (w_in[:, o[i]:o[i + 1]] for i in range(6))
    wqi = jnp.pad(wqi.reshape(d, IDX_HEADS, IDX_DIM), ((0, 0), (0, 0), (0, LANES - IDX_DIM)))
    wki = jnp.pad(wki, ((0, 0), (0, LANES - IDX_DIM)))
    wwi = jnp.pad(wwi, ((0, 0), (0, LANES - IDX_HEADS)))
    return jnp.concatenate([wq, wk, wv, wqi.reshape(d, IDX_HEADS * LANES), wki, wwi], axis=1).astype(BF16)


def _dsa_in(h, mix_g, positions, w_in):
    t = h.shape[0]
    tm = min(TM_MIX, t)
    spec = lambda w: pl.BlockSpec((tm, w), lambda i: (i, 0))
    widths = (N_HEADS * HEAD_DIM, KV_DIM, KV_DIM, IDX_HEADS * LANES, LANES)
    return pl.pallas_call(
        _dsa_in_kernel,
        grid=(t // tm,),
        in_specs=[spec(D_MODEL), _const_spec((1, D_MODEL)), spec(1), _const_spec((D_MODEL, _P_END)),
                  _const_spec((1, LANES)), _const_spec((1, LANES))],
        out_specs=[spec(w) for w in widths] + [spec(LANES)],
        out_shape=[jax.ShapeDtypeStruct((t, w), BF16) for w in widths]
        + [jax.ShapeDtypeStruct((t, LANES), F32)],
        scratch_shapes=[pltpu.VMEM((tm, D_MODEL), BF16)],
        compiler_params=_cparams("parallel"),
        name="dsa_in_proj",
    )(h, mix_g.reshape(1, -1), positions.reshape(t, 1), _dsa_in_weights(w_in),
      _inv_freq_lanes(ROT_DIM), _inv_freq_lanes(IDX_ROT_DIM))


_INT_MIN = -2 ** 31
_U_NEG_INF = 0x007FFFFF


def _key_to_f32(u):
    bits = jnp.where(u < 0, u ^ _INT_MIN, ~u)
    return lax.bitcast_convert_type(bits, F32)


def _dsa_core_kernel(q_ref, qi_ref, wi_ref, k_ref, v_ref, ki_ref, o_ref,
                     score_s, sel_s, bias_s, lg_s, mx_s, l_s, acc_s, *, k_top):
    qb = q_ref.shape[0]
    sk = k_ref.shape[0]
    n_fold = K_TILE // LANES
    q_end = (pl.program_id(1) + 1) * qb
    n_kt = (q_end + (K_TILE - 1)) // K_TILE
    neg_inf = jnp.float32(-jnp.inf)

    def tile_start(kt):
        return pl.multiple_of(kt * K_TILE, K_TILE)

    def for_tiles_grouped(tile_fn):
        def quad(i, c):
            for u in range(4):
                tile_fn(4 * i + u, c)
            return c

        n_quads = lax.shift_right_logical(n_kt, 2)
        lax.fori_loop(0, n_quads, quad, 0)
        rest = lax.bitwise_and(n_kt, 3)

        @pl.when(rest >= 2)
        def _():
            tile_fn(4 * n_quads, 0)
            tile_fn(4 * n_quads + 1, 0)

        @pl.when(lax.bitwise_and(rest, 1) == 1)
        def _():
            tile_fn(n_kt - 1, 0)

    fold_rows = 64

    def key_fold(x, op):
        return op(x.reshape(K_TILE // fold_rows, fold_rows, qb), axis=0)

    def kq_pos(k0):
        shape = (K_TILE, qb)
        return (k0 + lax.broadcasted_iota(I32, shape, 0),
                q_end - qb + lax.broadcasted_iota(I32, shape, 1))

    qi_all = jnp.concatenate([qi_ref[:, hd * LANES:(hd + 1) * LANES] for hd in range(IDX_HEADS)], axis=0)
    wi_t = wi_ref[...].T
    wrows = [wi_t[hd:hd + 1, :] for hd in range(IDX_HEADS)]

    def score_tile(kt, c):
        k0 = tile_start(kt)
        rel = _dot_nt(ki_ref[pl.ds(k0, K_TILE), :], qi_all)
        acc = jnp.zeros((K_TILE, qb), F32)
        for hd in range(IDX_HEADS):
            acc = acc + wrows[hd] * jnp.maximum(rel[:, hd * qb:(hd + 1) * qb], 0.0)
        kpos, qpos = kq_pos(k0)
        score_s[pl.ds(k0, K_TILE), :] = jnp.where(kpos <= qpos, acc, neg_inf)
        return c

    for_tiles_grouped(score_tile)

    def count_ge(thr):
        def body(kt, acc):
            tile = score_s[pl.ds(tile_start(kt), K_TILE), :]
            return acc + key_fold(jnp.where(tile >= thr, 1.0, 0.0), jnp.sum)

        acc = lax.fori_loop(0, n_kt, body, jnp.zeros((fold_rows, qb), F32))
        return jnp.sum(acc, axis=0, keepdims=True)

    kf = jnp.float32(k_top)

    def bisect(p, ans):
        bit = lax.shift_left(jnp.int32(1), 31 - p)
        cand = ans | bit
        force = (cand >= 0) & (cand <= _U_NEG_INF)
        ok = (count_ge(_key_to_f32(cand)) >= kf) | force
        return jnp.where(ok, cand, ans)

    ans = lax.fori_loop(0, 32, bisect, jnp.zeros((1, qb), I32))
    thr = _key_to_f32(jnp.where(ans == _U_NEG_INF, ans + 1, ans))
    thr_up = _key_to_f32(ans + 1)

    def count_bin(kt, accs):
        tile = score_s[pl.ds(tile_start(kt), K_TILE), :]
        return (accs[0] + key_fold(jnp.where(tile >= thr_up, 1.0, 0.0), jnp.sum),
                accs[1] + key_fold(jnp.where(tile >= thr, 1.0, 0.0), jnp.sum))

    zero = jnp.zeros((fold_rows, qb), F32)
    above, at_least = lax.fori_loop(0, n_kt, count_bin, (zero, zero))
    c_hi = jnp.sum(above, axis=0, keepdims=True)
    need_bin = kf - c_hi
    in_bin = jnp.sum(at_least, axis=0, keepdims=True) - c_hi
    fits = in_bin <= need_bin
    thr_take = jnp.where(fits, thr, thr_up)
    need0 = jnp.where(fits, 0.0, need_bin)
    rem0 = jnp.where(fits, 0.0, in_bin)

    def init_sel(kt, c):
        k0 = tile_start(kt)
        sel_s[pl.ds(k0, K_TILE), :] = jnp.where(score_s[pl.ds(k0, K_TILE), :] >= thr_take, 0.0, NEG_BIG)
        return c

    lax.fori_loop(0, n_kt, init_sel, 0)

    def active_of(need, rem):
        return (need > 0.0) & (rem > 0.0)

    def bin_cond(carry):
        return carry[2] > 0.0

    def bin_body(carry):
        need, rem, _ = carry
        active = active_of(need, rem)

        def bin_tile(k0):
            s = score_s[pl.ds(k0, K_TILE), :]
            return s, (s >= thr) & (sel_s[pl.ds(k0, K_TILE), :] != 0.0)

        def best_of(kt, best):
            s, in_bin = bin_tile(tile_start(kt))
            return jnp.maximum(best, key_fold(jnp.where(in_bin, s, neg_inf), jnp.max))

        best = lax.fori_loop(0, n_kt, best_of, jnp.full((fold_rows, qb), neg_inf, F32))
        m = jnp.max(best, axis=0, keepdims=True)

        def first_of(kt, first):
            k0 = tile_start(kt)
            s, in_bin = bin_tile(k0)
            kpos, _ = kq_pos(k0)
            return jnp.minimum(first, key_fold(jnp.where((s == m) & in_bin, kpos, sk), jnp.min))

        first = lax.fori_loop(0, n_kt, first_of, jnp.full((fold_rows, qb), sk, I32))
        pick = jnp.min(first, axis=0, keepdims=True)

        def mark(kt, c):
            k0 = tile_start(kt)
            kpos, _ = kq_pos(k0)
            sel_s[pl.ds(k0, K_TILE), :] = jnp.where(active & (kpos == pick), 0.0, sel_s[pl.ds(k0, K_TILE), :])
            return c

        lax.fori_loop(0, n_kt, mark, 0)
        step = jnp.where(active, 1.0, 0.0)
        need, rem = need - step, rem - step
        go = jnp.max(jnp.where(active_of(need, rem), 1.0, 0.0))
        return need, rem, go

    go0 = jnp.max(jnp.where(active_of(need0, rem0), 1.0, 0.0))
    lax.while_loop(bin_cond, bin_body, (need0, rem0, go0))

    def to_rows(kt, c):
        k0 = tile_start(kt)
        bias_s[:, pl.ds(k0, K_TILE)] = sel_s[pl.ds(k0, K_TILE), :].T
        return c

    lax.fori_loop(0, n_kt, to_rows, 0)

    scale = HEAD_DIM ** -0.5 * float(np.log2(np.e))

    def q_rows(g):
        return jnp.concatenate(
            [q_ref[:, (g * HEADS_PER_KV + r) * HEAD_DIM:(g * HEADS_PER_KV + r + 1) * HEAD_DIM]
             for r in range(HEADS_PER_KV)], axis=0)

    def fold(x, op):
        out = x[:, 0:LANES]
        for c in range(1, n_fold):
            out = op(out, x[:, c * LANES:(c + 1) * LANES])
        return out

    mx_s[...] = jnp.full(mx_s.shape, NEG_BIG, F32)
    l_s[...] = jnp.zeros(l_s.shape, F32)
    acc_s[...] = jnp.zeros(acc_s.shape, F32)

    def logit_tile(kt, c):
        k0 = tile_start(kt)
        bias = jnp.concatenate([bias_s[:, pl.ds(k0, K_TILE)]] * HEADS_PER_KV, axis=0)
        for g in range(N_KV_HEADS):
            kt_ = k_ref[pl.ds(k0, K_TILE), g * HEAD_DIM:(g + 1) * HEAD_DIM]
            logits = _dot_nt(q_rows(g), kt_) * scale + bias
            lg_s[g, :, pl.ds(k0, K_TILE)] = logits
            mx_s[g] = jnp.maximum(mx_s[g], fold(logits, jnp.maximum))
        return c

    for_tiles_grouped(logit_tile)
    row_max = [jnp.max(mx_s[g], axis=-1, keepdims=True) for g in range(N_KV_HEADS)]

    def att_tile(kt, c):
        k0 = tile_start(kt)
        for g in range(N_KV_HEADS):
            vt_ = v_ref[pl.ds(k0, K_TILE), g * HEAD_DIM:(g + 1) * HEAD_DIM]
            pexp = jnp.exp2(lg_s[g, :, pl.ds(k0, K_TILE)] - row_max[g])
            l_s[g] = l_s[g] + fold(pexp, jnp.add)
            acc_s[g] = acc_s[g] + _dot(pexp.astype(BF16), vt_)
        return c

    for_tiles_grouped(att_tile)
    for g in range(N_KV_HEADS):
        og = acc_s[g] / jnp.sum(l_s[g], axis=-1, keepdims=True)
        for r in range(HEADS_PER_KV):
            c0 = (g * HEADS_PER_KV + r) * HEAD_DIM
            o_ref[:, c0:c0 + HEAD_DIM] = og[r * qb:(r + 1) * qb, :].astype(BF16)


def _dsa_core(q, k, v, qi, ki, wi, batch, seq):
    qb = Q_BLOCK
    assert qb == LANES and seq % K_TILE == 0
    k_top = min(TOPK_MAX, seq // 4)
    view = lambda a: a.reshape(batch, seq, a.shape[-1])
    q3, k3, v3, qi3, ki3, wi3 = (view(a) for a in (q, k, v, qi, ki, wi))
    qspec = lambda w: pl.BlockSpec((None, qb, w), lambda b, j: (b, j, 0))
    kspec = lambda w: pl.BlockSpec((None, seq, w), lambda b, j: (b, 0, 0))
    rows = HEADS_PER_KV * qb
    out = pl.pallas_call(
        functools.partial(_dsa_core_kernel, k_top=k_top),
        grid=(batch, seq // qb),
        in_specs=[qspec(D_MODEL), qspec(IDX_HEADS * LANES), qspec(LANES),
                  kspec(KV_DIM), kspec(KV_DIM), kspec(LANES)],
        out_specs=qspec(D_MODEL),
        out_shape=jax.ShapeDtypeStruct((batch, seq, D_MODEL), BF16),
        scratch_shapes=[pltpu.VMEM((seq, qb), F32), pltpu.VMEM((seq, qb), F32),
                        pltpu.VMEM((qb, seq), F32),
                        pltpu.VMEM((N_KV_HEADS, rows, seq), F32),
                        pltpu.VMEM((N_KV_HEADS, rows, LANES), F32),
                        pltpu.VMEM((N_KV_HEADS, rows, LANES), F32),
                        pltpu.VMEM((N_KV_HEADS, rows, HEAD_DIM), F32)],
        compiler_params=_cparams("parallel", "arbitrary"),
        name="dsa_core",
    )(q3, qi3, wi3, k3, v3, ki3)
    return out.reshape(batch * seq, D_MODEL)


def _dsa_out_kernel(h_ref, a_ref, wo_ref, ffng_ref, wr_ref, br_ref, h_out_ref, xn_ref, route_ref,
                    cnt_ref):
    h = h_ref[...] + _dot(a_ref[...], wo_ref[...])
    h_out_ref[...] = h
    xn = _rms(h, ffng_ref[...])
    xn_ref[...] = xn
    route_ref[...], cnt_ref[...] = _route(xn, wr_ref, br_ref)


def _dsa_out(h, att, w_out, ffn_g, wr, br):
    t = h.shape[0]
    tm = min(TM_MIX, t)
    row_spec = pl.BlockSpec((tm, D_MODEL), lambda i: (i, 0))
    return pl.pallas_call(
        _dsa_out_kernel,
        grid=(t // tm,),
        in_specs=[row_spec, row_spec, _const_spec((D_MODEL, D_MODEL)), _const_spec((1, D_MODEL)),
                  _const_spec((D_MODEL, LANES)), _const_spec((1, LANES))],
        out_specs=[row_spec, row_spec, pl.BlockSpec((tm, LANES), lambda i: (i, 0)),
                   pl.BlockSpec((8, LANES), lambda i: (i, 0))],
        out_shape=[jax.ShapeDtypeStruct((t, D_MODEL), F32), jax.ShapeDtypeStruct((t, D_MODEL), F32),
                   jax.ShapeDtypeStruct((t, LANES), F32), jax.ShapeDtypeStruct((t // tm * 8, LANES), F32)],
        compiler_params=_cparams("parallel"),
        name="dsa_out_proj",
    )(h, att, w_out.astype(BF16), ffn_g.reshape(1, -1), wr, br)


def kernel(x, p, positions, mix_norm, a_w_in, a_v_gain, a_w_s, a_b_s, a_w_out, b_w_in, b_w_out,
           ffn_norm, moe_w_group, moe_b_group, moe_w_expert, moe_b_expert, moe_w1, moe_w3, moe_w2,
           ple_norm, ple_w_gate, ple_w_proj, final_norm):
    batch, seq, d = x.shape
    depth = mix_norm.shape[0]
    t = batch * seq
    h = x.reshape(t, d)
    for i in range(depth):
        j = i // 2
        wr, br = _router_weights(moe_w_group[i], moe_b_group[i], moe_w_expert[i], moe_b_expert[i])
        if i % 2 == 0:
            h, xn, route, tile_counts = _gmlp_layer(h, mix_norm[i], a_w_in[j], a_v_gain[j], a_w_s[j],
                                                    a_b_s[j], a_w_out[j], ffn_norm[i], wr, br)
        else:
            q, k, v, qi, ki, wi = _dsa_in(h, mix_norm[i], positions, b_w_in[j])
            att = _dsa_core(q, k, v, qi, ki, wi, batch, seq)
            h, xn, route, tile_counts = _dsa_out(h, att, b_w_out[j], ffn_norm[i], wr, br)
        bufy, dest = _moe(xn, route, tile_counts, moe_w1, moe_w3, moe_w2, i)
        h = _moe_combine_ple(h, route, bufy, dest, p.reshape(depth, t, -1), i, ple_norm[i],
                             ple_w_gate[i], ple_w_proj[i], final_norm, final=(i == depth - 1))
    return h.reshape(batch, seq, d)
```
